```python
import math
import jax, jax.numpy as jnp
from jax import lax
import numpy as np


D_MODEL = 1024
BATCH = 4
SEQ = 8192
DEPTH = 2

D_FF = 2816
NORM_EPS = 1e-6

HY_WIDTH = D_MODEL // 4
HY_ORDER = 2
HY_SHORT = 3
HY_BANDS = 16
HY_EMB = 1 + 2 * HY_BANDS
HY_FILTER_HIDDEN = 64
HY_FAST_DECAY_PCT = 0.3
HY_SLOW_DECAY_PCT = 1.5
HY_DECAY_TARGET = 1e-2
HY_DECAY_SHIFT = 0.0
HY_IN = (HY_ORDER + 1) * HY_WIDTH

MLA_HEADS = 4
MLA_QK_NOPE = 128
MLA_QK_ROPE = 64
MLA_V_HEAD = 128
MLA_Q_LORA = D_MODEL // 4
MLA_KV_LORA = D_MODEL // 8
MLA_WIDTH = MLA_HEADS * MLA_V_HEAD
ROPE_BASE = 10000.0
Q_BLOCK = 128

GLA_HEADS = 4
GLA_WIDTH = D_MODEL // 4
GLA_KEY = GLA_WIDTH // 2
GLA_DK = GLA_KEY // GLA_HEADS
GLA_DV = GLA_WIDTH // GLA_HEADS
GLA_GATE_RANK = 16
GLA_GATE_NORM = 16.0
GLA_CHUNK = 64

MIX_WIDTH = HY_WIDTH + MLA_WIDTH + GLA_WIDTH
IN_SPLITS = (HY_IN, MLA_Q_LORA, MLA_KV_LORA, MLA_QK_ROPE, GLA_KEY, GLA_KEY, GLA_WIDTH, GLA_GATE_RANK, GLA_GATE_RANK, GLA_WIDTH)
IN_WIDTH = sum(IN_SPLITS)

kernel_name = 'hybrid_hyena_mla_gla_macaron_encoder'


def rms_norm(x, g):
    xf = x.astype(jnp.float32)
    y = xf * lax.rsqrt(jnp.mean(xf * xf, axis=-1, keepdims=True) + NORM_EPS)
    return (y * g.astype(jnp.float32)).astype(x.dtype)


def swiglu_ffn(x, w_gate, w_up, w_down):
    return (jax.nn.silu(x @ w_gate) * (x @ w_up)) @ w_down


def centred_short_conv(u, w, b):
    L = u.shape[1]
    pad = HY_SHORT // 2
    up = jnp.pad(u, ((0, 0), (pad, HY_SHORT - 1 - pad), (0, 0)))
    y = b
    for tap in range(HY_SHORT):
        y = y + up[:, tap:tap + L] * w[tap]
    return y


def hyena_filters(L, w1, b1, f1, w2, b2, f2, w3):
    pos = jnp.arange(L, dtype=jnp.float32)
    t = pos / (L - 1)
    bands = jnp.linspace(1e-4, HY_BANDS - 1, HY_BANDS, dtype=jnp.float32)
    ang = (2.0 * math.pi * pos / L)[:, None] * bands[None, :]
    z = jnp.concatenate([t[:, None], jnp.cos(ang), -jnp.sin(ang)], axis=-1)
    h = jnp.sin(f1 * (z @ w1 + b1))
    h = jnp.sin(f2 * (h @ w2 + b2))
    h = (h @ w3).astype(jnp.float32).reshape(L, HY_ORDER, 2, HY_WIDTH)
    max_decay = math.log(HY_DECAY_TARGET) / HY_FAST_DECAY_PCT
    min_decay = math.log(HY_DECAY_TARGET) / HY_SLOW_DECAY_PCT
    deltas = jnp.abs(jnp.linspace(min_decay, max_decay, HY_WIDTH, dtype=jnp.float32))
    window = jnp.exp(-t[:, None] * deltas[None, :]) + HY_DECAY_SHIFT
    h = h * window[:, None, None, :]
    h_fwd = h[:, :, 0]
    h_bwd = h[1:, :, 1][::-1]
    return jnp.concatenate([h_fwd, jnp.zeros_like(h_fwd[:1]), h_bwd], axis=0)


def fft_long_conv(u, kern, skip):
    L = u.shape[1]
    U = jnp.fft.rfft(u, n=2 * L, axis=1)
    K = jnp.fft.rfft(kern, n=2 * L, axis=0)
    y = jnp.fft.irfft(U * K[None], n=2 * L, axis=1)[:, :L]
    return y + u * skip.astype(jnp.float32)


def hyena_mixer(u_proj, conv_w, conv_b, w1, b1, f1, w2, b2, f2, w3, skip, out_norm):
    dtype = u_proj.dtype
    L = u_proj.shape[1]
    u = centred_short_conv(u_proj, conv_w, conv_b).astype(jnp.float32)
    v, x1, x2 = jnp.split(u, HY_ORDER + 1, axis=-1)
    kern = hyena_filters(L, w1, b1, f1, w2, b2, f2, w3)
    z = x1 * fft_long_conv(v, kern[:, 0], skip[0])
    y = x2 * fft_long_conv(z, kern[:, 1], skip[1])
    return rms_norm(y.astype(dtype), out_norm)


def rope(x, pos):
    half = x.shape[-1] // 2
    inv = ROPE_BASE ** (-jnp.arange(half, dtype=jnp.float32) * 2.0 / x.shape[-1])
    ang = pos[:, None] * inv[None, :]
    cos = jnp.cos(ang)[None, :, None, :]
    sin = jnp.sin(ang)[None, :, None, :]
    xf = x.astype(jnp.float32)
    a, b = xf[..., :half], xf[..., half:]
    return jnp.concatenate([a * cos - b * sin, a * sin + b * cos], axis=-1).astype(x.dtype)


def mla_mixer(cq_in, ckv_in, kr_in, q_norm, w_uq, kv_norm, w_ukv, out_norm):
    B, S, _ = cq_in.shape
    dqk = MLA_QK_NOPE + MLA_QK_ROPE
    pos = jnp.arange(S, dtype=jnp.float32)
    q = (rms_norm(cq_in, q_norm) @ w_uq).reshape(B, S, MLA_HEADS, dqk)
    q = jnp.concatenate([q[..., :MLA_QK_NOPE], rope(q[..., MLA_QK_NOPE:], pos)], axis=-1)
    kv = (rms_norm(ckv_in, kv_norm) @ w_ukv).reshape(B, S, MLA_HEADS, MLA_QK_NOPE + MLA_V_HEAD)
    k_nope, v = kv[..., :MLA_QK_NOPE], kv[..., MLA_QK_NOPE:]
    k_rope = rope(kr_in[:, :, None, :], pos)
    k = jnp.concatenate([k_nope, jnp.broadcast_to(k_rope, (B, S, MLA_HEADS, MLA_QK_ROPE))], axis=-1)
    scale = dqk ** -0.5
    qb = q.reshape(B, S // Q_BLOCK, Q_BLOCK, MLA_HEADS, dqk).transpose(1, 0, 2, 3, 4)

    def attend_block(q_blk):
        s = jnp.einsum('bqhd,bkhd->bhqk', q_blk, k).astype(jnp.float32) * scale
        p = jax.nn.softmax(s, axis=-1).astype(v.dtype)
        return jnp.einsum('bhqk,bkhe->bqhe', p, v)

    o = lax.map(attend_block, qb)
    o = o.transpose(1, 0, 2, 3, 4).reshape(B, S, MLA_WIDTH)
    return rms_norm(o, out_norm)


def gla_chunk_scan(q, k, v, g, inclusive):
    B, S, H, dk = q.shape
    dv = v.shape[-1]
    n_chunks = S // GLA_CHUNK

    def to_chunks(t):
        return t.reshape(B, n_chunks, GLA_CHUNK, H, t.shape[-1]).transpose(1, 0, 3, 2, 4)

    mask = jnp.tril(jnp.ones((GLA_CHUNK, GLA_CHUNK), dtype=bool), k=0 if inclusive else -1)

    def step(state, inp):
        qc, kc, vc, gc = inp
        b = lax.cumsum(gc, axis=2)
        o_inter = jnp.einsum('bhcd,bhde->bhce', qc * jnp.exp(b), state)
        diff = b[:, :, :, None, :] - b[:, :, None, :, :]
        decay = jnp.exp(jnp.where(mask[:, :, None], diff, -jnp.inf))
        attn = jnp.einsum('bhid,bhjd,bhijd->bhij', qc, kc, decay)
        o_intra = jnp.einsum('bhij,bhje->bhie', attn, vc)
        b_last = b[:, :, -1:, :]
        state = jnp.exp(b_last[:, :, 0, :, None]) * state + jnp.einsum('bhcd,bhce->bhde', kc * jnp.exp(b_last - b), vc)
        return state, o_inter + o_intra

    init = jnp.zeros((B, H, dk, dv), jnp.float32)
    _, o = lax.scan(step, init, (to_chunks(q), to_chunks(k), to_chunks(v), to_chunks(g)))
    return o.transpose(1, 0, 3, 2, 4).reshape(B, S, H, dv)


def gla_mixer(q_in, k_in, v_in, gf_in, gb_in, r_in, w_gf, b_gf, w_gb, b_gb, head_norm):
    dtype = q_in.dtype
    B, S, _ = q_in.shape
    f32 = jnp.float32

    def heads(t, d):
        return t.astype(f32).reshape(B, S, GLA_HEADS, d)

    q = heads(q_in, GLA_DK) * (GLA_DK ** -0.5)
    k = heads(k_in, GLA_DK)
    v = heads(v_in, GLA_DV)
    g_f = heads(jax.nn.log_sigmoid(gf_in.astype(f32) @ w_gf.astype(f32) + b_gf.astype(f32)) / GLA_GATE_NORM, GLA_DK)
    g_b = heads(jax.nn.log_sigmoid(gb_in.astype(f32) @ w_gb.astype(f32) + b_gb.astype(f32)) / GLA_GATE_NORM, GLA_DK)
    o_f = gla_chunk_scan(q, k, v, g_f, True)
    o_b = gla_chunk_scan(q[:, ::-1], k[:, ::-1], v[:, ::-1], g_b[:, ::-1], False)[:, ::-1]
    o = rms_norm(o_f + o_b, head_norm).reshape(B, S, GLA_WIDTH)
    return (o * jax.nn.silu(r_in.astype(f32))).astype(dtype)


def hybrid_layer(x, ffn1_norm, ffn1_w_gate, ffn1_w_up, ffn1_w_down, mix_norm, w_in,
                 hy_conv_w, hy_conv_b, hy_filt_w1, hy_filt_b1, hy_filt_freq1, hy_filt_w2, hy_filt_b2,
                 hy_filt_freq2, hy_filt_w3, hy_skip, hy_out_norm,
                 mla_q_norm, mla_w_uq, mla_kv_norm, mla_w_ukv, mla_out_norm,
                 gla_w_gate_fwd, gla_b_gate_fwd, gla_w_gate_bwd, gla_b_gate_bwd, gla_head_norm,
                 w_out, ffn2_norm, ffn2_w_gate, ffn2_w_up, ffn2_w_down):
    x = x + 0.5 * swiglu_ffn(rms_norm(x, ffn1_norm), ffn1_w_gate, ffn1_w_up, ffn1_w_down)
    proj = rms_norm(x, mix_norm) @ w_in
    offsets = [int(o) for o in np.cumsum(IN_SPLITS)[:-1]]
    (hy_in, mla_cq, mla_ckv, mla_kr, gla_q, gla_k, gla_v, gla_gf, gla_gb, gla_r) = jnp.split(proj, offsets, axis=-1)
    y_hy = hyena_mixer(hy_in, hy_conv_w, hy_conv_b, hy_filt_w1, hy_filt_b1, hy_filt_freq1, hy_filt_w2,
                       hy_filt_b2, hy_filt_freq2, hy_filt_w3, hy_skip, hy_out_norm)
    y_mla = mla_mixer(mla_cq, mla_ckv, mla_kr, mla_q_norm, mla_w_uq, mla_kv_norm, mla_w_ukv, mla_out_norm)
    y_gla = gla_mixer(gla_q, gla_k, gla_v, gla_gf, gla_gb, gla_r, gla_w_gate_fwd, gla_b_gate_fwd,
                      gla_w_gate_bwd, gla_b_gate_bwd, gla_head_norm)
    x = x + jnp.concatenate([y_hy, y_mla, y_gla], axis=-1) @ w_out
    x = x + 0.5 * swiglu_ffn(rms_norm(x, ffn2_norm), ffn2_w_gate, ffn2_w_up, ffn2_w_down)
    return x


def setup_inputs(seed: int = 0) -> dict:
    key = jax.random.key(seed)
    ks = iter(jax.random.split(key, 40))

    def nrm(shape, scale):
        return scale * jax.random.normal(next(ks), shape, jnp.float32)

    def gain(shape):
        return 1.0 + nrm(shape, 0.02)

    L = DEPTH
    Hf = HY_FILTER_HIDDEN
    return {
        'x': nrm((BATCH, SEQ, D_MODEL), 1.0),
        'ffn1_norm': gain((L, D_MODEL)),
        'ffn1_w_gate': nrm((L, D_MODEL, D_FF), D_MODEL ** -0.5),
        'ffn1_w_up': nrm((L, D_MODEL, D_FF), D_MODEL ** -0.5),
        'ffn1_w_down': nrm((L, D_FF, D_MODEL), D_FF ** -0.5),
        'mix_norm': gain((L, D_MODEL)),
        'w_in': nrm((L, D_MODEL, IN_WIDTH), D_MODEL ** -0.5),
        'hy_conv_w': nrm((L, HY_SHORT, HY_IN), HY_SHORT ** -0.5),
        'hy_conv_b': nrm((L, HY_IN), 0.02),
        'hy_filt_w1': nrm((L, HY_EMB, Hf), HY_EMB ** -0.5),
        'hy_filt_b1': nrm((L, Hf), 0.02),
        'hy_filt_freq1': gain((L, Hf)),
        'hy_filt_w2': nrm((L, Hf, Hf), Hf ** -0.5),
        'hy_filt_b2': nrm((L, Hf), 0.02),
        'hy_filt_freq2': gain((L, Hf)),
        'hy_filt_w3': nrm((L, Hf, HY_ORDER * 2 * HY_WIDTH), Hf ** -0.5),
        'hy_skip': nrm((L, HY_ORDER, HY_WIDTH), 1.0),
        'hy_out_norm': gain((L, HY_WIDTH)),
        'mla_q_norm': gain((L, MLA_Q_LORA)),
        'mla_w_uq': nrm((L, MLA_Q_LORA, MLA_HEADS * (MLA_QK_NOPE + MLA_QK_ROPE)), MLA_Q_LORA ** -0.5),
        'mla_kv_norm': gain((L, MLA_KV_LORA)),
        'mla_w_ukv': nrm((L, MLA_KV_LORA, MLA_HEADS * (MLA_QK_NOPE + MLA_V_HEAD)), MLA_KV_LORA ** -0.5),
        'mla_out_norm': gain((L, MLA_WIDTH)),
        'gla_w_gate_fwd': nrm((L, GLA_GATE_RANK, GLA_KEY), GLA_GATE_RANK ** -0.5),
        'gla_b_gate_fwd': nrm((L, GLA_KEY), 0.1),
        'gla_w_gate_bwd': nrm((L, GLA_GATE_RANK, GLA_KEY), GLA_GATE_RANK ** -0.5),
        'gla_b_gate_bwd': nrm((L, GLA_KEY), 0.1),
        'gla_head_norm': gain((L, GLA_DV)),
        'w_out': nrm((L, MIX_WIDTH, D_MODEL), MIX_WIDTH ** -0.5),
        'ffn2_norm': gain((L, D_MODEL)),
        'ffn2_w_gate': nrm((L, D_MODEL, D_FF), D_MODEL ** -0.5),
        'ffn2_w_up': nrm((L, D_MODEL, D_FF), D_MODEL ** -0.5),
        'ffn2_w_down': nrm((L, D_FF, D_MODEL), D_FF ** -0.5),
        'final_norm': gain((D_MODEL,)),
    }


def reference(x, ffn1_norm, ffn1_w_gate, ffn1_w_up, ffn1_w_down, mix_norm, w_in,
              hy_conv_w, hy_conv_b, hy_filt_w1, hy_filt_b1, hy_filt_freq1, hy_filt_w2, hy_filt_b2,
              hy_filt_freq2, hy_filt_w3, hy_skip, hy_out_norm,
              mla_q_norm, mla_w_uq, mla_kv_norm, mla_w_ukv, mla_out_norm,
              gla_w_gate_fwd, gla_b_gate_fwd, gla_w_gate_bwd, gla_b_gate_bwd, gla_head_norm,
              w_out, ffn2_norm, ffn2_w_gate, ffn2_w_up, ffn2_w_down, final_norm):
    for l in range(DEPTH):
        x = hybrid_layer(x, ffn1_norm[l], ffn1_w_gate[l], ffn1_w_up[l], ffn1_w_down[l], mix_norm[l], w_in[l],
                         hy_conv_w[l], hy_conv_b[l], hy_filt_w1[l], hy_filt_b1[l], hy_filt_freq1[l],
                         hy_filt_w2[l], hy_filt_b2[l], hy_filt_freq2[l], hy_filt_w3[l], hy_skip[l], hy_out_norm[l],
                         mla_q_norm[l], mla_w_uq[l], mla_kv_norm[l], mla_w_ukv[l], mla_out_norm[l],
                         gla_w_gate_fwd[l], gla_b_gate_fwd[l], gla_w_gate_bwd[l], gla_b_gate_bwd[l], gla_head_norm[l],
                         w_out[l], ffn2_norm[l], ffn2_w_gate[l], ffn2_w_up[l], ffn2_w_down[l])
    return rms_norm(x, final_norm)
```

```python
import functools
import math

import jax
import jax.numpy as jnp
import numpy as np
from jax import lax
from jax.experimental import pallas as pl
from jax.experimental.pallas import tpu as pltpu

F32 = jnp.float32
BF16 = jnp.bfloat16
HIGHEST = lax.Precision.HIGHEST

NORM_EPS = 1e-6
HY_ORDER = 2
HY_SHORT = 3
HY_BANDS = 16
HY_FAST_DECAY_PCT = 0.3
HY_SLOW_DECAY_PCT = 1.5
HY_DECAY_TARGET = 1e-2
HY_DECAY_SHIFT = 0.0
MLA_HEADS = 4
MLA_QK_NOPE = 128
MLA_QK_ROPE = 64
MLA_V_HEAD = 128
ROPE_BASE = 10000.0
GLA_HEADS = 4
GLA_GATE_RANK = 16
GLA_GATE_NORM = 16.0

LANES = 128
GLA_CHUNK = 64
GLA_SUB = 16
VMEM_LIMIT = 56 * 1024 * 1024

NEG_BIG = -1e30


def _cparams(*sem):
    return pltpu.CompilerParams(dimension_semantics=sem, vmem_limit_bytes=VMEM_LIMIT)


def _rms(x, g):
    ms = jnp.mean(x * x, axis=-1, keepdims=True)
    return x * lax.rsqrt(ms + NORM_EPS) * g


def _const_spec(shape):
    nd = len(shape)
    return pl.BlockSpec(shape, lambda *_: (0,) * nd)


def _ffn_kernel(x_ref, g_ref, wg_ref, wu_ref, wd_ref, fg_ref, o_ref, *, ff_chunk, final):
    x = x_ref[...]
    xn = _rms(x, g_ref[...]).astype(BF16)
    d_ff = wg_ref.shape[1]
    acc = jnp.zeros_like(x)
    for c in range(0, d_ff, ff_chunk):
        gate = jnp.dot(xn, wg_ref[:, c:c + ff_chunk], preferred_element_type=F32)
        up = jnp.dot(xn, wu_ref[:, c:c + ff_chunk], preferred_element_type=F32)
        h = (gate * jax.nn.sigmoid(gate) * up).astype(BF16)
        acc = acc + jnp.dot(h, wd_ref[c:c + ff_chunk, :], preferred_element_type=F32)
    y = x + 0.5 * acc
    if final:
        y = _rms(y, fg_ref[...])
    o_ref[...] = y


def _ffn(x2d, norm_g, w_gate, w_up, w_down, final_g, *, final, tm=512, ff_chunk=256):
    t, d = x2d.shape
    d_ff = w_gate.shape[1]
    kern = functools.partial(_ffn_kernel, ff_chunk=ff_chunk, final=final)
    return pl.pallas_call(
        kern,
        grid=(t // tm,),
        in_specs=[
            pl.BlockSpec((tm, d), lambda i: (i, 0)),
            _const_spec((1, d)),
            _const_spec((d, d_ff)),
            _const_spec((d, d_ff)),
            _const_spec((d_ff, d)),
            _const_spec((1, d)),
        ],
        out_specs=pl.BlockSpec((tm, d), lambda i: (i, 0)),
        out_shape=jax.ShapeDtypeStruct((t, d), F32),
        compiler_params=_cparams("parallel"),
        name="ffn",
    )(x2d, norm_g.reshape(1, d), w_gate.astype(BF16), w_up.astype(BF16), w_down.astype(BF16),
      final_g.reshape(1, d))


def _log_sigmoid(z):
    return jnp.minimum(z, 0.0) - jnp.log(1.0 + jnp.exp(-jnp.abs(z)))


def _in_proj_kernel(x_ref, g_ref, win_ref, qn_ref, wuq_ref, kvn_ref, wukv_ref, cos_ref, sin_ref,
                    wgate_ref, bgate_ref,
                    hy_ref, q_ref, k_ref, v_ref, gq_ref, gk_ref, gv_ref, gf_ref, gb_ref, gr_ref,
                    *, off, q_scale, gla_q_scale):
    x = x_ref[0]
    xn = _rms(x, g_ref[...]).astype(BF16)
    proj = jnp.dot(xn, win_ref[...], preferred_element_type=F32)

    def blk(name):
        lo, hi = off[name]
        return proj[:, lo:hi]

    hy_ref[0] = blk("hy")

    qn = _rms(blk("cq"), qn_ref[...]).astype(BF16)
    qf = jnp.dot(qn, wuq_ref[...], preferred_element_type=F32)
    kvn = _rms(blk("ckv"), kvn_ref[...]).astype(BF16)
    kvf = jnp.dot(kvn, wukv_ref[...], preferred_element_type=F32)
    cos = cos_ref[...]
    sin = sin_ref[...]
    k_rope = blk("kr") * cos + blk("kr_rot") * sin
    ones = jnp.ones_like(k_rope)
    qw = 3 * LANES
    kvw = 2 * LANES
    for h in range(MLA_HEADS):
        q_nope = qf[:, h * qw:h * qw + LANES]
        q_rope = (qf[:, h * qw + LANES:h * qw + 2 * LANES] * cos
                  + qf[:, h * qw + 2 * LANES:(h + 1) * qw] * sin)
        q_ref[0, h] = (jnp.concatenate([q_nope, q_rope], axis=-1) * q_scale).astype(BF16)
        k_ref[0, h] = jnp.concatenate([kvf[:, h * kvw:h * kvw + LANES], k_rope], axis=-1).astype(BF16)
        v_ref[0, h] = jnp.concatenate([kvf[:, h * kvw + LANES:(h + 1) * kvw], ones], axis=-1).astype(BF16)

    gq_ref[0] = blk("gq") * gla_q_scale
    gk_ref[0] = blk("gk")
    gv_ref[0] = blk("gv")
    z = jnp.dot(blk("gate"), wgate_ref[...], precision=HIGHEST, preferred_element_type=F32) + bgate_ref[...]
    gates = _log_sigmoid(z) / GLA_GATE_NORM
    gk_w = gf_ref.shape[-1]
    gf_ref[0] = gates[:, :gk_w]
    gb_ref[0] = gates[:, gk_w:]
    gr_ref[0] = blk("gr")


def _rope_rot_cols(w):
    half = w.shape[-1] // 2
    return jnp.concatenate([-w[..., half:], w[..., :half]], axis=-1)


def _pad_cols(w, width):
    return jnp.pad(w, ((0, 0), (0, width - w.shape[-1])))


def _in_proj(x, mix_g, w_in, q_norm, w_uq, kv_norm, w_ukv, w_gf, b_gf, w_gb, b_gb, *, tm=512):
    b, s, d = x.shape
    hy_in = 3 * (d // 4)
    q_lora = d // 4
    kv_lora = d // 8
    gla_w = d // 4
    gla_key = gla_w // 2
    rope = MLA_QK_ROPE
    splits = (hy_in, q_lora, kv_lora, rope, gla_key, gla_key, gla_w, GLA_GATE_RANK, GLA_GATE_RANK, gla_w)
    offs = np.concatenate([[0], np.cumsum(splits)])
    (w_hy, w_cq, w_ckv, w_kr, w_gq, w_gk, w_gv, w_gfl, w_gbl, w_gr) = [
        w_in[:, offs[i]:offs[i + 1]] for i in range(len(splits))]
    pieces = [
        ("hy", w_hy), ("cq", w_cq), ("ckv", w_ckv),
        ("kr", _pad_cols(w_kr, LANES)), ("kr_rot", _pad_cols(_rope_rot_cols(w_kr), LANES)),
        ("gq", w_gq), ("gk", w_gk), ("gv", w_gv),
        ("gate", _pad_cols(jnp.concatenate([w_gfl, w_gbl], axis=1), LANES)),
        ("gr", w_gr),
    ]
    off = {}
    pos = 0
    for name, w in pieces:
        off[name] = (pos, pos + w.shape[1])
        pos += w.shape[1]
    win_r = jnp.concatenate([w for _, w in pieces], axis=1).astype(BF16)
    n_in = win_r.shape[1]

    dqk = MLA_QK_NOPE + MLA_QK_ROPE
    wq = w_uq.reshape(q_lora, MLA_HEADS, dqk)
    wq_nope = wq[..., :MLA_QK_NOPE]
    wq_rope = wq[..., MLA_QK_NOPE:]
    zpad = jnp.zeros((q_lora, MLA_HEADS, LANES - rope), F32)
    wuq_r = jnp.concatenate([wq_nope, wq_rope, zpad, _rope_rot_cols(wq_rope), zpad], axis=-1)
    wuq_r = wuq_r.reshape(q_lora, MLA_HEADS * 3 * LANES).astype(BF16)
    wukv_r = w_ukv.astype(BF16)

    half = rope // 2
    inv = ROPE_BASE ** (-jnp.arange(half, dtype=F32) * 2.0 / rope)
    ang = jnp.arange(s, dtype=F32)[:, None] * inv[None, :]
    zt = jnp.zeros((s, LANES - rope), F32)
    cos_t = jnp.concatenate([jnp.cos(ang), jnp.cos(ang), zt], axis=1)
    sin_t = jnp.concatenate([jnp.sin(ang), jnp.sin(ang), zt], axis=1)

    wgate = jnp.zeros((LANES, 2 * gla_key), F32)
    wgate = wgate.at[:GLA_GATE_RANK, :gla_key].set(w_gf)
    wgate = wgate.at[GLA_GATE_RANK:2 * GLA_GATE_RANK, gla_key:].set(w_gb)
    bgate = jnp.concatenate([b_gf, b_gb]).reshape(1, 2 * gla_key)

    kern = functools.partial(_in_proj_kernel, off=off, q_scale=float(dqk) ** -0.5,
                             gla_q_scale=float(gla_key // GLA_HEADS) ** -0.5)
    tok = lambda w: pl.BlockSpec((1, tm, w), lambda bi, i: (bi, i, 0))
    head = lambda w: pl.BlockSpec((1, MLA_HEADS, tm, w), lambda bi, i: (bi, 0, i, 0))
    outs = pl.pallas_call(
        kern,
        grid=(b, s // tm),
        in_specs=[
            tok(d),
            _const_spec((1, d)),
            _const_spec((d, n_in)),
            _const_spec((1, q_lora)),
            _const_spec((q_lora, MLA_HEADS * 3 * LANES)),
            _const_spec((1, kv_lora)),
            _const_spec((kv_lora, MLA_HEADS * 2 * LANES)),
            pl.BlockSpec((tm, LANES), lambda bi, i: (i, 0)),
            pl.BlockSpec((tm, LANES), lambda bi, i: (i, 0)),
            _const_spec((LANES, 2 * gla_key)),
            _const_spec((1, 2 * gla_key)),
        ],
        out_specs=[
            tok(hy_in),
            head(2 * LANES), head(2 * LANES), head(2 * LANES),
            tok(gla_key), tok(gla_key), tok(gla_w), tok(gla_key), tok(gla_key), tok(gla_w),
        ],
        out_shape=[
            jax.ShapeDtypeStruct((b, s, hy_in), F32),
            jax.ShapeDtypeStruct((b, MLA_HEADS, s, 2 * LANES), BF16),
            jax.ShapeDtypeStruct((b, MLA_HEADS, s, 2 * LANES), BF16),
            jax.ShapeDtypeStruct((b, MLA_HEADS, s, 2 * LANES), BF16),
            jax.ShapeDtypeStruct((b, s, gla_key), F32),
            jax.ShapeDtypeStruct((b, s, gla_key), F32),
            jax.ShapeDtypeStruct((b, s, gla_w), F32),
            jax.ShapeDtypeStruct((b, s, gla_key), F32),
            jax.ShapeDtypeStruct((b, s, gla_key), F32),
            jax.ShapeDtypeStruct((b, s, gla_w), F32),
        ],
        compiler_params=_cparams("parallel", "parallel"),
        name="in_proj",
    )(x, mix_g.reshape(1, d), win_r, q_norm.reshape(1, -1), wuq_r, kv_norm.reshape(1, -1), wukv_r,
      cos_t, sin_t, wgate, bgate)
    return outs


def _attn_kernel(q_ref, k_ref, v_ref, o_ref, *, tk):
    q = q_ref[...]
    tq = q.shape[0]
    s_len = k_ref.shape[0]
    dv = o_ref.shape[-1]

    def body(j, carry):
        m, acc = carry
        start = pl.multiple_of(j * tk, tk)
        k = k_ref[pl.ds(start, tk), :]
        v = v_ref[pl.ds(start, tk), :]
        s = lax.dot_general(q, k, (((1,), (1,)), ((), ())), preferred_element_type=F32)
        m_new = jnp.maximum(m, jnp.max(s, axis=-1, keepdims=True))
        alpha = jnp.exp(m - m_new)
        p = jnp.exp(s - m_new).astype(BF16)
        acc = acc * alpha + jnp.dot(p, v, preferred_element_type=F32)
        return m_new, acc

    m0 = jnp.full((tq, 1), -jnp.inf, F32)
    acc0 = jnp.zeros((tq, v_ref.shape[-1]), F32)
    _, acc = lax.fori_loop(0, s_len // tk, body, (m0, acc0))
    o_ref[...] = acc[:, :dv] / acc[:, dv:]


def _mla_attention(q, k, v, *, tq=512, tk=512):
    b, h, s, dq = q.shape
    dv = MLA_V_HEAD
    kern = functools.partial(_attn_kernel, tk=tk)
    return pl.pallas_call(
        kern,
        grid=(b, h, s // tq),
        in_specs=[
            pl.BlockSpec((None, None, tq, dq), lambda bi, hi, i: (bi, hi, i, 0)),
            pl.BlockSpec((None, None, s, dq), lambda bi, hi, i: (bi, hi, 0, 0)),
            pl.BlockSpec((None, None, s, 2 * dv), lambda bi, hi, i: (bi, hi, 0, 0)),
        ],
        out_specs=pl.BlockSpec((None, tq, dv), lambda bi, hi, i: (bi, i, hi)),
        out_shape=jax.ShapeDtypeStruct((b, s, h * dv), F32),
        compiler_params=_cparams("parallel", "parallel", "parallel"),
        name="mla_attn",
    )(q, k, v)


def _gla_chunk(q, k, v, g, tri, e3, hmask, vmask, *, reverse):
    c = q.shape[0]
    nsub = c // GLA_SUB
    beta = jnp.dot(tri, g, precision=HIGHEST, preferred_element_type=F32)
    tot = beta[0:1] if reverse else beta[c - 1:c]
    qd = q * jnp.exp(beta)
    kd = k * jnp.exp(tot - beta)
    ut = lax.dot_general(v.astype(BF16), kd.astype(BF16), (((0,), (0,)), ((), ())),
                         preferred_element_type=F32)
    dec = jnp.exp(tot)

    jj = lax.broadcasted_iota(jnp.int32, (GLA_SUB, GLA_SUB, LANES), 0)
    ii = lax.broadcasted_iota(jnp.int32, (GLA_SUB, GLA_SUB, LANES), 1)
    keep = (jj > ii) if reverse else (jj <= ii)

    v16 = v.astype(BF16)
    outs = []
    for sb in range(nsub):
        r0, r1 = sb * GLA_SUB, (sb + 1) * GLA_SUB
        qs, ks, bs, vs = q[r0:r1], k[r0:r1], beta[r0:r1], v[r0:r1]
        diff = jnp.where(keep, bs[None, :, :] - bs[:, None, :], NEG_BIG)
        p = (qs[None, :, :] * ks[:, None, :]) * jnp.exp(diff)
        w = jnp.dot(p.reshape(GLA_SUB * GLA_SUB, LANES).astype(BF16), e3, preferred_element_type=F32)
        o_sb = jnp.sum(w.reshape(GLA_SUB, GLA_SUB, w.shape[-1]) * vs[:, None, :], axis=0)
        if reverse:
            j0, j1 = r1, c
            ref = beta[r1:r1 + 1] if r1 < c else None
        else:
            j0, j1 = 0, r0
            ref = beta[r0 - 1:r0] if r0 > 0 else None
        if ref is not None:
            qt = qs * jnp.exp(bs - ref)
            kt = (k[j0:j1] * jnp.exp(ref - beta[j0:j1])).astype(BF16)
            q_stack = jnp.concatenate([qt * hmask[hh:hh + 1] for hh in range(GLA_HEADS)], axis=0).astype(BF16)
            a = lax.dot_general(q_stack, kt, (((1,), (1,)), ((), ())), preferred_element_type=F32)
            o_all = jnp.dot(a.astype(BF16), v16[j0:j1], preferred_element_type=F32)
            for hh in range(GLA_HEADS):
                o_sb = o_sb + o_all[hh * GLA_SUB:(hh + 1) * GLA_SUB] * vmask[hh:hh + 1]
        outs.append(o_sb)
    intra = jnp.concatenate(outs, axis=0)
    return intra, qd, ut, dec


def _gla_kernel(qf_ref, kf_ref, vf_ref, gf_ref, qb_ref, kb_ref, vb_ref, gb_ref,
                tril_ref, triu_ref, e3_ref, hmask_ref, vmask_ref, stmask_ref,
                of_ref, ob_ref, stf_ref, stb_ref):
    @pl.when(pl.program_id(1) == 0)
    def _():
        stf_ref[...] = jnp.zeros_like(stf_ref)
        stb_ref[...] = jnp.zeros_like(stb_ref)

    tb = qf_ref.shape[0]
    nc = tb // GLA_CHUNK
    e3 = e3_ref[...]
    hmask = hmask_ref[...]
    vmask = vmask_ref[...]
    stmask = stmask_ref[...]

    def step(c, carry):
        for reverse in (False, True):
            if reverse:
                q_ref, k_ref, v_ref, g_ref, o_ref, st_ref, tri = (
                    qb_ref, kb_ref, vb_ref, gb_ref, ob_ref, stb_ref, triu_ref[...])
                r0 = pl.multiple_of((nc - 1 - c) * GLA_CHUNK, GLA_CHUNK)
            else:
                q_ref, k_ref, v_ref, g_ref, o_ref, st_ref, tri = (
                    qf_ref, kf_ref, vf_ref, gf_ref, of_ref, stf_ref, tril_ref[...])
                r0 = pl.multiple_of(c * GLA_CHUNK, GLA_CHUNK)
            rows = pl.ds(r0, GLA_CHUNK)
            intra, qd, ut, dec = _gla_chunk(q_ref[rows, :], k_ref[rows, :], v_ref[rows, :], g_ref[rows, :],
                                            tri, e3, hmask, vmask, reverse=reverse)
            st = st_ref[...]
            inter = lax.dot_general(qd.astype(BF16), st.astype(BF16), (((1,), (1,)), ((), ())),
                                    preferred_element_type=F32)
            o_ref[rows, :] = intra + inter
            st_ref[...] = st * dec + ut * stmask
        return carry

    lax.fori_loop(0, nc, step, 0)


def _gla(gq, gk, gv, gf, gb, *, tb=512):
    b, s, kw = gq.shape
    vw = gv.shape[-1]
    nb = s // tb
    dk = kw // GLA_HEADS
    dv = vw // GLA_HEADS
    idx = np.arange(GLA_CHUNK)
    tril = jnp.asarray((idx[None, :] <= idx[:, None]).astype(np.float32))
    triu = jnp.asarray((idx[None, :] >= idx[:, None]).astype(np.float32))
    d_head = np.arange(kw) // dk
    e_head = np.arange(vw) // dv
    e3 = jnp.asarray((d_head[:, None] == e_head[None, :]).astype(np.float32)).astype(BF16)
    hmask = jnp.asarray((np.arange(GLA_HEADS)[:, None] == d_head[None, :]).astype(np.float32))
    vmask = jnp.asarray((np.arange(GLA_HEADS)[:, None] == e_head[None, :]).astype(np.float32))
    stmask = jnp.asarray((e_head[:, None] == d_head[None, :]).astype(np.float32))

    fwd = lambda w: pl.BlockSpec((None, tb, w), lambda bi, i: (bi, i, 0))
    bwd = lambda w: pl.BlockSpec((None, tb, w), lambda bi, i: (bi, nb - 1 - i, 0))
    return pl.pallas_call(
        _gla_kernel,
        grid=(b, nb),
        in_specs=[
            fwd(kw), fwd(kw), fwd(vw), fwd(kw),
            bwd(kw), bwd(kw), bwd(vw), bwd(kw),
            _const_spec((GLA_CHUNK, GLA_CHUNK)), _const_spec((GLA_CHUNK, GLA_CHUNK)),
            _const_spec((kw, vw)), _const_spec((GLA_HEADS, kw)), _const_spec((GLA_HEADS, vw)),
            _const_spec((vw, kw)),
        ],
        out_specs=[fwd(vw), bwd(vw)],
        out_shape=[jax.ShapeDtypeStruct((b, s, vw), F32), jax.ShapeDtypeStruct((b, s, vw), F32)],
        scratch_shapes=[pltpu.VMEM((vw, kw), F32), pltpu.VMEM((vw, kw), F32)],
        compiler_params=_cparams("parallel", "arbitrary"),
        name="gla",
    )(gq, gk, gv, gf, gq, gk, gv, gb, tril, triu, e3, hmask, vmask, stmask)


def _short_conv_kernel(x_ref, w_ref, b_ref, o_ref):
    x = x_ref[...]
    n = x.shape[0]
    row = lax.broadcasted_iota(jnp.int32, x.shape, 0)
    prev = jnp.where(row == 0, 0.0, pltpu.roll(x, 1, axis=0))
    nxt = jnp.where(row == n - 1, 0.0, pltpu.roll(x, n - 1, axis=0))
    w = w_ref[...]
    o_ref[...] = b_ref[...] + prev * w[0:1] + x * w[1:2] + nxt * w[2:3]


def _short_conv(hy_in, conv_w, conv_b):
    b, s, c3 = hy_in.shape
    c = c3 // 3
    nt = c // LANES
    return pl.pallas_call(
        _short_conv_kernel,
        grid=(b, c3 // LANES),
        in_specs=[
            pl.BlockSpec((None, s, LANES), lambda bi, j: (bi, 0, j)),
            pl.BlockSpec((HY_SHORT, LANES), lambda bi, j: (0, j)),
            pl.BlockSpec((1, LANES), lambda bi, j: (0, j)),
        ],
        out_specs=pl.BlockSpec((None, None, s, LANES), lambda bi, j: (j // nt, bi, 0, j % nt)),
        out_shape=jax.ShapeDtypeStruct((3, b, s, c), F32),
        compiler_params=_cparams("parallel", "parallel"),
        name="hy_short_conv",
    )(hy_in, conv_w, conv_b.reshape(1, c3))


def _filter_kernel(band_ref, w1_ref, b1_ref, f1_ref, w2_ref, b2_ref, f2_ref, w3_ref, delta_ref, o_ref,
                   *, seq):
    tr = o_ref.shape[1]
    c = delta_ref.shape[-1]
    n = pl.program_id(0) * tr + lax.broadcasted_iota(jnp.int32, (tr, LANES), 0)
    lane = lax.broadcasted_iota(jnp.int32, (tr, LANES), 1)
    pos = jnp.where(n < seq, n, 2 * seq - n).astype(F32)
    t = pos / (seq - 1)
    ang = (2.0 * math.pi * pos / seq) * band_ref[...]
    z = jnp.where(lane == 0, t,
                  jnp.where(lane <= HY_BANDS, jnp.cos(ang),
                            jnp.where(lane <= 2 * HY_BANDS, -jnp.sin(ang), 0.0)))
    h = jnp.sin(f1_ref[...] * (jnp.dot(z, w1_ref[...], precision=HIGHEST, preferred_element_type=F32)
                               + b1_ref[...]))
    h = jnp.sin(f2_ref[...] * (jnp.dot(h, w2_ref[...], precision=HIGHEST, preferred_element_type=F32)
                               + b2_ref[...]))
    h = jnp.dot(h, w3_ref[...], precision=HIGHEST, preferred_element_type=F32)
    t_c = jnp.concatenate([t] * (c // LANES), axis=-1)
    window = jnp.exp(-t_c * delta_ref[...]) + HY_DECAY_SHIFT
    live = jnp.concatenate([n] * (c // LANES), axis=-1) != seq
    for o in range(HY_ORDER):
        o_ref[o] = jnp.where(live, h[:, o * c:(o + 1) * c] * window, 0.0)


def _hyena_filters(seq, w1, b1, f1, w2, b2, f2, w3, *, tr=512):
    emb, hid = w1.shape
    c = w3.shape[1] // (HY_ORDER * 2)
    n = 2 * seq
    bands = np.zeros((1, LANES), np.float32)
    bvals = np.linspace(1e-4, HY_BANDS - 1, HY_BANDS, dtype=np.float32)
    bands[0, 1:1 + HY_BANDS] = bvals
    bands[0, 1 + HY_BANDS:1 + 2 * HY_BANDS] = bvals
    max_decay = math.log(HY_DECAY_TARGET) / HY_FAST_DECAY_PCT
    min_decay = math.log(HY_DECAY_TARGET) / HY_SLOW_DECAY_PCT
    deltas = np.abs(np.linspace(min_decay, max_decay, c, dtype=np.float32)).reshape(1, c)
    w1p = jnp.pad(w1, ((0, LANES - emb), (0, 0)))
    w3d = w3.reshape(hid, HY_ORDER, 2, c).transpose(2, 0, 1, 3).reshape(2, hid, HY_ORDER * c)
    half_tiles = seq // tr
    kern = functools.partial(_filter_kernel, seq=seq)
    return pl.pallas_call(
        kern,
        grid=(n // tr,),
        in_specs=[
            _const_spec((1, LANES)),
            _const_spec((LANES, hid)), _const_spec((1, hid)), _const_spec((1, hid)),
            _const_spec((hid, hid)), _const_spec((1, hid)), _const_spec((1, hid)),
            pl.BlockSpec((None, hid, HY_ORDER * c), lambda i: (i // half_tiles, 0, 0)),
            _const_spec((1, c)),
        ],
        out_specs=pl.BlockSpec((HY_ORDER, tr, c), lambda i: (0, i, 0)),
        out_shape=jax.ShapeDtypeStruct((HY_ORDER, n, c), F32),
        compiler_params=_cparams("parallel"),
        name="hy_filter",
    )(jnp.asarray(bands), w1p, b1.reshape(1, hid), f1.reshape(1, hid), w2, b2.reshape(1, hid),
      f2.reshape(1, hid), w3d, jnp.asarray(deltas))


def _dft_tables(p):
    idx = np.arange(p)
    ang = 2.0 * np.pi * np.outer(idx, idx) / p
    fr, fi = np.cos(ang), -np.sin(ang)
    tang = 2.0 * np.pi * np.outer(idx, idx) / (p * p)
    twr, twi = np.cos(tang), -np.sin(tang)
    return fr, fi, twr, twi


def _stage_a_kernel(x_ref, m_ref, o_ref):
    x = jnp.concatenate([x_ref[0], x_ref[1]], axis=0)
    o_ref[0] = jnp.dot(m_ref[...], x, precision=HIGHEST, preferred_element_type=F32)


def _stage_a(x4, idx, mat, *, ct=2048):
    _, g2, hp, cols = x4.shape
    rows = mat.shape[0]
    return pl.pallas_call(
        _stage_a_kernel,
        grid=(g2 // 2, cols // ct),
        in_specs=[
            pl.BlockSpec((None, 2, hp, ct), lambda g, j: (idx, g, 0, j)),
            _const_spec(mat.shape),
        ],
        out_specs=pl.BlockSpec((1, rows, ct), lambda g, j: (g, 0, j)),
        out_shape=jax.ShapeDtypeStruct((g2 // 2, rows, cols), F32),
        compiler_params=_cparams("parallel", "parallel"),
        name="hy_dft_a",
    )(x4, mat)


def _fwd_matrix(fr, fi, twr, twi):
    gr = fr * twr - fi * twi
    gi = fr * twi + fi * twr
    return jnp.concatenate([jnp.concatenate([gr, -gi], axis=1), jnp.concatenate([gi, gr], axis=1)], axis=0)


def _stack_pairs(a_ref, j):
    g = a_ref.shape[0]
    re = jnp.concatenate([a_ref[i, 0, j] for i in range(g)], axis=-1)
    im = jnp.concatenate([a_ref[i, 1, j] for i in range(g)], axis=-1)
    return jnp.concatenate([re, im], axis=0)


def _stage_c_kernel(a_ref, fr_ref, fi_ref, twr_ref, twi_ref, o_ref):
    g, _, kk, p, c = a_ref.shape
    fr, fi = fr_ref[...], fi_ref[...]
    for j in range(kk):
        gs = _fwd_matrix(fr, fi, twr_ref[j], twi_ref[j])
        x = jnp.dot(gs, _stack_pairs(a_ref, j), precision=HIGHEST, preferred_element_type=F32)
        for i in range(g):
            o_ref[i, 0, j] = x[:p, i * c:(i + 1) * c]
            o_ref[i, 1, j] = x[p:, i * c:(i + 1) * c]


def _conv_core_kernel(a_ref, kh_ref, fr_ref, fi_ref, twr_ref, twi_ref, twbr_ref, twbi_ref, o_ref):
    g, _, kk, p, c = a_ref.shape
    fr, fi = fr_ref[...], fi_ref[...]
    for j in range(kk):
        gs = _fwd_matrix(fr, fi, twr_ref[j], twi_ref[j])
        x = jnp.dot(gs, _stack_pairs(a_ref, j), precision=HIGHEST, preferred_element_type=F32)
        xr, xi = x[:p], x[p:]
        kr = jnp.concatenate([kh_ref[0, j]] * g, axis=-1)
        ki = jnp.concatenate([kh_ref[1, j]] * g, axis=-1)
        y = jnp.concatenate([xr * kr - xi * ki, xr * ki + xi * kr], axis=0)
        br = fr * twbr_ref[j] - fi * twbi_ref[j]
        bi = -(fr * twbi_ref[j] + fi * twbr_ref[j])
        bs = jnp.concatenate([jnp.concatenate([br, -bi], axis=1), jnp.concatenate([bi, br], axis=1)], axis=0)
        r = jnp.dot(bs, y, precision=HIGHEST, preferred_element_type=F32)
        for i in range(g):
            o_ref[i, 0, j] = r[:p, i * c:(i + 1) * c]
            o_ref[i, 1, j] = r[p:, i * c:(i + 1) * c]


def _tw_specs(p, kk):
    row = pl.BlockSpec((kk, 1, p), lambda i: (i, 0, 0))
    return row


def _stage_c(a5, tabs, *, kk=4):
    g, _, p, _, c = a5.shape
    fr, fi, twr, twi = tabs
    blk = pl.BlockSpec((g, 2, kk, p, c), lambda i: (0, 0, i, 0, 0))
    return pl.pallas_call(
        _stage_c_kernel,
        grid=(p // kk,),
        in_specs=[blk, _const_spec((p, p)), _const_spec((p, p)), _tw_specs(p, kk), _tw_specs(p, kk)],
        out_specs=blk,
        out_shape=jax.ShapeDtypeStruct(a5.shape, F32),
        compiler_params=_cparams("parallel"),
        name="hy_dft_c",
    )(a5, fr, fi, twr.reshape(p, 1, p), twi.reshape(p, 1, p))


def _conv_core(a5, khat, tabs, twb, *, kk=4):
    g, _, p, _, c = a5.shape
    fr, fi, twr, twi = tabs
    twbr, twbi = twb
    blk = pl.BlockSpec((g, 2, kk, p, c), lambda i: (0, 0, i, 0, 0))
    return pl.pallas_call(
        _conv_core_kernel,
        grid=(p // kk,),
        in_specs=[
            blk,
            pl.BlockSpec((2, kk, p, c), lambda i: (0, i, 0, 0)),
            _const_spec((p, p)), _const_spec((p, p)), _tw_specs(p, kk), _tw_specs(p, kk),
            pl.BlockSpec((kk, p, p), lambda i: (i, 0, 0)),
            pl.BlockSpec((kk, p, p), lambda i: (i, 0, 0)),
        ],
        out_specs=blk,
        out_shape=jax.ShapeDtypeStruct(a5.shape, F32),
        compiler_params=_cparams("parallel"),
        name="hy_conv_core",
    )(a5, khat, fr, fi, twr.reshape(p, 1, p), twi.reshape(p, 1, p), twbr, twbi)


def _stage_b_kernel(r_ref, m_ref, u_ref, gate_ref, skip_ref, o_ref):
    y = jnp.dot(m_ref[...], r_ref[0], precision=HIGHEST, preferred_element_type=F32)
    hp = u_ref.shape[1]
    skip = skip_ref[...]
    o_ref[0] = gate_ref[0] * (y[:hp] + skip * u_ref[0])
    o_ref[1] = gate_ref[1] * (y[hp:] + skip * u_ref[1])


def _stage_b(r3, mat, u4, u_idx, gate4, gate_idx, skip_row, *, ct=2048):
    g, rows, cols = r3.shape
    hp = mat.shape[0] // 2
    nb = u4.shape[1]
    return pl.pallas_call(
        _stage_b_kernel,
        grid=(g, cols // ct),
        in_specs=[
            pl.BlockSpec((1, rows, ct), lambda gi, j: (gi, 0, j)),
            _const_spec(mat.shape),
            pl.BlockSpec((None, 2, hp, ct), lambda gi, j: (u_idx, gi, 0, j)),
            pl.BlockSpec((None, 2, hp, ct), lambda gi, j: (gate_idx, gi, 0, j)),
            pl.BlockSpec((1, ct), lambda gi, j: (0, j)),
        ],
        out_specs=pl.BlockSpec((2, hp, ct), lambda gi, j: (gi, 0, j)),
        out_shape=jax.ShapeDtypeStruct((nb, hp, cols), F32),
        compiler_params=_cparams("parallel", "parallel"),
        name="hy_dft_b",
    )(r3, mat, u4, gate4, skip_row)


def _hyena(hy_in, conv_w, conv_b, w1, b1, f1, w2, b2, f2, w3, skip):
    b, s, c3 = hy_in.shape
    c = c3 // 3
    p = math.isqrt(2 * s)
    assert p * p == 2 * s and b % 2 == 0
    hp = p // 2
    cols = p * c
    fr, fi, twr, twi = _dft_tables(p)
    scale = 1.0 / (p * p)
    mat_a = np.block([[fr[:, :hp], -fi[:, :hp]], [fi[:, :hp], fr[:, :hp]]]).astype(np.float32)
    mat_k = np.concatenate([fr, fi], axis=0).astype(np.float32)
    br, bi = fr[:hp] * scale, -fi[:hp] * scale
    mat_b = np.block([[br, -bi], [bi, br]]).astype(np.float32)
    tabs = tuple(jnp.asarray(t.astype(np.float32)) for t in (fr, fi, twr, twi))
    twb = tuple(jnp.broadcast_to(t[:, :, None], (p, p, p)) for t in tabs[2:])

    uvx = _short_conv(hy_in, conv_w, conv_b).reshape(3, b, hp, cols)
    kern = _hyena_filters(s, w1, b1, f1, w2, b2, f2, w3)
    ka = _stage_a(kern.reshape(1, HY_ORDER * 2, hp, cols), 0, jnp.asarray(mat_k))
    khat = _stage_c(ka.reshape(HY_ORDER, 2, p, p, c), tabs)

    skip_rows = jnp.tile(skip, (1, p))
    g = b // 2

    def long_conv(u4, u_idx, gate_idx, order):
        a = _stage_a(u4, u_idx, jnp.asarray(mat_a))
        r = _conv_core(a.reshape(g, 2, p, p, c), khat[order], tabs, twb)
        return _stage_b(r.reshape(g, 2 * p, cols), jnp.asarray(mat_b), u4, u_idx, uvx, gate_idx,
                        skip_rows[order:order + 1])

    z = long_conv(uvx, 0, 1, 0)
    y = long_conv(z[None], 0, 2, 1)
    return y.reshape(b, s, c)


def _out_proj_kernel(x_ref, hy_ref, mla_ref, of_ref, ob_ref, r_ref, ghy_ref, gmla_ref, ggla_ref,
                     seg_ref, w_ref, o_ref):
    y_hy = _rms(hy_ref[...], ghy_ref[...])
    y_mla = _rms(mla_ref[...], gmla_ref[...])
    o = of_ref[...] + ob_ref[...]
    ms = jnp.dot(o * o, seg_ref[...], precision=HIGHEST, preferred_element_type=F32)
    r = r_ref[...]
    y_gla = o * lax.rsqrt(ms + NORM_EPS) * ggla_ref[...] * (r * jax.nn.sigmoid(r))
    cat = jnp.concatenate([y_hy, y_mla, y_gla], axis=-1).astype(BF16)
    o_ref[...] = x_ref[...] + jnp.dot(cat, w_ref[...], preferred_element_type=F32)


def _out_proj(x2d, y_hy, o_mla, o_f, o_b, r, g_hy, g_mla, g_head, w_out, *, tm=512):
    t, d = x2d.shape
    c_hy, c_mla, c_gla = y_hy.shape[1], o_mla.shape[1], o_f.shape[1]
    dv = c_gla // GLA_HEADS
    head = np.arange(c_gla) // dv
    seg = jnp.asarray((head[:, None] == head[None, :]).astype(np.float32) / dv)
    tok = lambda w: pl.BlockSpec((tm, w), lambda i: (i, 0))
    return pl.pallas_call(
        _out_proj_kernel,
        grid=(t // tm,),
        in_specs=[
            tok(d), tok(c_hy), tok(c_mla), tok(c_gla), tok(c_gla), tok(c_gla),
            _const_spec((1, c_hy)), _const_spec((1, c_mla)), _const_spec((1, c_gla)),
            _const_spec((c_gla, c_gla)), _const_spec(w_out.shape),
        ],
        out_specs=tok(d),
        out_shape=jax.ShapeDtypeStruct((t, d), F32),
        compiler_params=_cparams("parallel"),
        name="out_proj",
    )(x2d, y_hy, o_mla, o_f, o_b, r, g_hy.reshape(1, -1), g_mla.reshape(1, -1),
      jnp.tile(g_head, GLA_HEADS).reshape(1, -1), seg, w_out.astype(BF16))


def kernel(x, ffn1_norm, ffn1_w_gate, ffn1_w_up, ffn1_w_down, mix_norm, w_in, hy_conv_w, hy_conv_b, hy_filt_w1, hy_filt_b1, hy_filt_freq1, hy_filt_w2, hy_filt_b2, hy_filt_freq2, hy_filt_w3, hy_skip, hy_out_norm, mla_q_norm, mla_w_uq, mla_kv_norm, mla_w_ukv, mla_out_norm, gla_w_gate_fwd, gla_b_gate_fwd, gla_w_gate_bwd, gla_b_gate_bwd, gla_head_norm, w_out, ffn2_norm, ffn2_w_gate, ffn2_w_up, ffn2_w_down, final_norm):
    b, s, d = x.shape
    depth = ffn1_norm.shape[0]
    t = b * s
    for l in range(depth):
        x2 = _ffn(x.reshape(t, d), ffn1_norm[l], ffn1_w_gate[l], ffn1_w_up[l], ffn1_w_down[l], final_norm,
                  final=False)
        (hy_in, q, k, v, gq, gk, gv, gf, gb, gr) = _in_proj(
            x2.reshape(b, s, d), mix_norm[l], w_in[l], mla_q_norm[l], mla_w_uq[l], mla_kv_norm[l],
            mla_w_ukv[l], gla_w_gate_fwd[l], gla_b_gate_fwd[l], gla_w_gate_bwd[l], gla_b_gate_bwd[l])
        y_hy = _hyena(hy_in, hy_conv_w[l], hy_conv_b[l], hy_filt_w1[l], hy_filt_b1[l], hy_filt_freq1[l],
                      hy_filt_w2[l], hy_filt_b2[l], hy_filt_freq2[l], hy_filt_w3[l], hy_skip[l])
        o_mla = _mla_attention(q, k, v)
        o_f, o_b = _gla(gq, gk, gv, gf, gb)
        x3 = _out_proj(x2, y_hy.reshape(t, -1), o_mla.reshape(t, -1), o_f.reshape(t, -1),
                       o_b.reshape(t, -1), gr.reshape(t, -1), hy_out_norm[l], mla_out_norm[l],
                       gla_head_norm[l], w_out[l])
        x = _ffn(x3, ffn2_norm[l], ffn2_w_gate[l], ffn2_w_up[l], ffn2_w_down[l], final_norm,
                 final=(l == depth - 1)).reshape(b, s, d)
    return x
```

```python
import functools
import math

import jax
import jax.numpy as jnp
import numpy as np
from jax import lax
from jax.experimental import pallas as pl
from jax.experimental.pallas import tpu as pltpu

F32 = jnp.float32
BF16 = jnp.bfloat16
HIGHEST = lax.Precision.HIGHEST

NORM_EPS = 1e-6
HY_ORDER = 2
HY_SHORT = 3
HY_BANDS = 16
HY_FAST_DECAY_PCT = 0.3
HY_SLOW_DECAY_PCT = 1.5
HY_DECAY_TARGET = 1e-2
HY_DECAY_SHIFT = 0.0
MLA_HEADS = 4
MLA_QK_NOPE = 128
MLA_QK_ROPE = 64
MLA_V_HEAD = 128
ROPE_BASE = 10000.0
GLA_HEADS = 4
GLA_GATE_RANK = 16
GLA_GATE_NORM = 16.0

LANES = 128
GLA_CHUNK = 64
GLA_SUB = 16
VMEM_LIMIT = 56 * 1024 * 1024

NEG_BIG = -1e30


def _cparams(*sem):
    return pltpu.CompilerParams(dimension_semantics=sem, vmem_limit_bytes=VMEM_LIMIT)


def _rms(x, g):
    ms = jnp.mean(x * x, axis=-1, keepdims=True)
    return x * lax.rsqrt(ms + NORM_EPS) * g


def _const_spec(shape):
    nd = len(shape)
    return pl.BlockSpec(shape, lambda *_: (0,) * nd)


def _ffn_kernel(x_ref, g_ref, wg_ref, wu_ref, wd_ref, fg_ref, o_ref, *, ff_chunk, final):
    x = x_ref[...]
    xn = _rms(x, g_ref[...]).astype(BF16)
    d_ff = wg_ref.shape[1]
    acc = jnp.zeros_like(x)
    for c in range(0, d_ff, ff_chunk):
        gate = jnp.dot(xn, wg_ref[:, c:c + ff_chunk], preferred_element_type=F32)
        up = jnp.dot(xn, wu_ref[:, c:c + ff_chunk], preferred_element_type=F32)
        h = (gate * jax.nn.sigmoid(gate) * up).astype(BF16)
        acc = acc + jnp.dot(h, wd_ref[c:c + ff_chunk, :], preferred_element_type=F32)
    y = x + 0.5 * acc
    if final:
        y = _rms(y, fg_ref[...])
    o_ref[...] = y


def _ffn(x2d, norm_g, w_gate, w_up, w_down, final_g, *, final, tm=512, ff_chunk=256):
    t, d = x2d.shape
    d_ff = w_gate.shape[1]
    kern = functools.partial(_ffn_kernel, ff_chunk=ff_chunk, final=final)
    return pl.pallas_call(
        kern,
        grid=(t // tm,),
        in_specs=[
            pl.BlockSpec((tm, d), lambda i: (i, 0)),
            _const_spec((1, d)),
            _const_spec((d, d_ff)),
            _const_spec((d, d_ff)),
            _const_spec((d_ff, d)),
            _const_spec((1, d)),
        ],
        out_specs=pl.BlockSpec((tm, d), lambda i: (i, 0)),
        out_shape=jax.ShapeDtypeStruct((t, d), F32),
        compiler_params=_cparams("parallel"),
        name="ffn",
    )(x2d, norm_g.reshape(1, d), w_gate.astype(BF16), w_up.astype(BF16), w_down.astype(BF16),
      final_g.reshape(1, d))


def _log_sigmoid(z):
    return jnp.minimum(z, 0.0) - jnp.log(1.0 + jnp.exp(-jnp.abs(z)))


def _in_proj_kernel(x_ref, g_ref, win_ref, qn_ref, wuq_ref, kvn_ref, wukv_ref, cos_ref, sin_ref,
                    wgate_ref, bgate_ref,
                    hy_ref, q_ref, k_ref, v_ref, gq_ref, gk_ref, gv_ref, gf_ref, gb_ref, gr_ref,
                    *, off, q_scale, gla_q_scale):
    x = x_ref[0]
    xn = _rms(x, g_ref[...]).astype(BF16)
    proj = jnp.dot(xn, win_ref[...], preferred_element_type=F32)

    def blk(name):
        lo, hi = off[name]
        return proj[:, lo:hi]

    hy_ref[0] = blk("hy")

    qn = _rms(blk("cq"), qn_ref[...]).astype(BF16)
    qf = jnp.dot(qn, wuq_ref[...], preferred_element_type=F32)
    kvn = _rms(blk("ckv"), kvn_ref[...]).astype(BF16)
    kvf = jnp.dot(kvn, wukv_ref[...], preferred_element_type=F32)
    cos = cos_ref[...]
    sin = sin_ref[...]
    k_rope = blk("kr") * cos + blk("kr_rot") * sin
    ones = jnp.ones_like(k_rope)
    qw = 3 * LANES
    kvw = 2 * LANES
    for h in range(MLA_HEADS):
        q_nope = qf[:, h * qw:h * qw + LANES]
        q_rope = (qf[:, h * qw + LANES:h * qw + 2 * LANES] * cos
                  + qf[:, h * qw + 2 * LANES:(h + 1) * qw] * sin)
        q_ref[0, h] = (jnp.concatenate([q_nope, q_rope], axis=-1) * q_scale).astype(BF16)
        k_ref[0, h] = jnp.concatenate([kvf[:, h * kvw:h * kvw + LANES], k_rope], axis=-1).astype(BF16)
        v_ref[0, h] = jnp.concatenate([kvf[:, h * kvw + LANES:(h + 1) * kvw], ones], axis=-1).astype(BF16)

    gq_ref[0] = blk("gq") * gla_q_scale
    gk_ref[0] = blk("gk")
    gv_ref[0] = blk("gv")
    z = jnp.dot(blk("gate"), wgate_ref[...], precision=HIGHEST, preferred_element_type=F32) + bgate_ref[...]
    gates = _log_sigmoid(z) / GLA_GATE_NORM
    gk_w = gf_ref.shape[-1]
    gf_ref[0] = gates[:, :gk_w]
    gb_ref[0] = gates[:, gk_w:]
    gr_ref[0] = blk("gr")


def _rope_rot_cols(w):
    half = w.shape[-1] // 2
    return jnp.concatenate([-w[..., half:], w[..., :half]], axis=-1)


def _pad_cols(w, width):
    return jnp.pad(w, ((0, 0), (0, width - w.shape[-1])))


def _in_proj(x, mix_g, w_in, q_norm, w_uq, kv_norm, w_ukv, w_gf, b_gf, w_gb, b_gb, *, tm=512):
    b, s, d = x.shape
    hy_in = 3 * (d // 4)
    q_lora = d // 4
    kv_lora = d // 8
    gla_w = d // 4
    gla_key = gla_w // 2
    rope = MLA_QK_ROPE
    splits = (hy_in, q_lora, kv_lora, rope, gla_key, gla_key, gla_w, GLA_GATE_RANK, GLA_GATE_RANK, gla_w)
    offs = np.concatenate([[0], np.cumsum(splits)])
    (w_hy, w_cq, w_ckv, w_kr, w_gq, w_gk, w_gv, w_gfl, w_gbl, w_gr) = [
        w_in[:, offs[i]:offs[i + 1]] for i in range(len(splits))]
    pieces = [
        ("hy", w_hy), ("cq", w_cq), ("ckv", w_ckv),
        ("kr", _pad_cols(w_kr, LANES)), ("kr_rot", _pad_cols(_rope_rot_cols(w_kr), LANES)),
        ("gq", w_gq), ("gk", w_gk), ("gv", w_gv),
        ("gate", _pad_cols(jnp.concatenate([w_gfl, w_gbl], axis=1), LANES)),
        ("gr", w_gr),
    ]
    off = {}
    pos = 0
    for name, w in pieces:
        off[name] = (pos, pos + w.shape[1])
        pos += w.shape[1]
    win_r = jnp.concatenate([w for _, w in pieces], axis=1).astype(BF16)
    n_in = win_r.shape[1]

    dqk = MLA_QK_NOPE + MLA_QK_ROPE
    wq = w_uq.reshape(q_lora, MLA_HEADS, dqk)
    wq_nope = wq[..., :MLA_QK_NOPE]
    wq_rope = wq[..., MLA_QK_NOPE:]
    zpad = jnp.zeros((q_lora, MLA_HEADS, LANES - rope), F32)
    wuq_r = jnp.concatenate([wq_nope, wq_rope, zpad, _rope_rot_cols(wq_rope), zpad], axis=-1)
    wuq_r = wuq_r.reshape(q_lora, MLA_HEADS * 3 * LANES).astype(BF16)
    wukv_r = w_ukv.astype(BF16)

    half = rope // 2
    inv = ROPE_BASE ** (-jnp.arange(half, dtype=F32) * 2.0 / rope)
    ang = jnp.arange(s, dtype=F32)[:, None] * inv[None, :]
    zt = jnp.zeros((s, LANES - rope), F32)
    cos_t = jnp.concatenate([jnp.cos(ang), jnp.cos(ang), zt], axis=1)
    sin_t = jnp.concatenate([jnp.sin(ang), jnp.sin(ang), zt], axis=1)

    wgate = jnp.zeros((LANES, 2 * gla_key), F32)
    wgate = wgate.at[:GLA_GATE_RANK, :gla_key].set(w_gf)
    wgate = wgate.at[GLA_GATE_RANK:2 * GLA_GATE_RANK, gla_key:].set(w_gb)
    bgate = jnp.concatenate([b_gf, b_gb]).reshape(1, 2 * gla_key)

    kern = functools.partial(_in_proj_kernel, off=off, q_scale=float(dqk) ** -0.5 * math.log2(math.e),
                             gla_q_scale=float(gla_key // GLA_HEADS) ** -0.5)
    tok = lambda w: pl.BlockSpec((1, tm, w), lambda bi, i: (bi, i, 0))
    head = lambda w: pl.BlockSpec((1, MLA_HEADS, tm, w), lambda bi, i: (bi, 0, i, 0))
    outs = pl.pallas_call(
        kern,
        grid=(b, s // tm),
        in_specs=[
            tok(d),
            _const_spec((1, d)),
            _const_spec((d, n_in)),
            _const_spec((1, q_lora)),
            _const_spec((q_lora, MLA_HEADS * 3 * LANES)),
            _const_spec((1, kv_lora)),
            _const_spec((kv_lora, MLA_HEADS * 2 * LANES)),
            pl.BlockSpec((tm, LANES), lambda bi, i: (i, 0)),
            pl.BlockSpec((tm, LANES), lambda bi, i: (i, 0)),
            _const_spec((LANES, 2 * gla_key)),
            _const_spec((1, 2 * gla_key)),
        ],
        out_specs=[
            tok(hy_in),
            head(2 * LANES), head(2 * LANES), head(2 * LANES),
            tok(gla_key), tok(gla_key), tok(gla_w), tok(gla_key), tok(gla_key), tok(gla_w),
        ],
        out_shape=[
            jax.ShapeDtypeStruct((b, s, hy_in), F32),
            jax.ShapeDtypeStruct((b, MLA_HEADS, s, 2 * LANES), BF16),
            jax.ShapeDtypeStruct((b, MLA_HEADS, s, 2 * LANES), BF16),
            jax.ShapeDtypeStruct((b, MLA_HEADS, s, 2 * LANES), BF16),
            jax.ShapeDtypeStruct((b, s, gla_key), F32),
            jax.ShapeDtypeStruct((b, s, gla_key), F32),
            jax.ShapeDtypeStruct((b, s, gla_w), F32),
            jax.ShapeDtypeStruct((b, s, gla_key), F32),
            jax.ShapeDtypeStruct((b, s, gla_key), F32),
            jax.ShapeDtypeStruct((b, s, gla_w), F32),
        ],
        compiler_params=_cparams("parallel", "parallel"),
        name="in_proj",
    )(x, mix_g.reshape(1, d), win_r, q_norm.reshape(1, -1), wuq_r, kv_norm.reshape(1, -1), wukv_r,
      cos_t, sin_t, wgate, bgate)
    return outs


def _attn_kernel(q_ref, k_ref, v_ref, o_ref, s0_ref, s1_ref, p_ref, acc_ref, m_ref, alpha_ref, *, tk, rb):
    q = q_ref[...]
    tq = q.shape[0]
    nk = k_ref.shape[0] // tk
    dv = o_ref.shape[-1]

    def scores(j, dst_ref):
        start = pl.multiple_of(j * tk, tk)
        dst_ref[...] = lax.dot_general(q, k_ref[pl.ds(start, tk), :], (((1,), (1,)), ((), ())),
                                       preferred_element_type=F32)

    def consume(j, cur_ref, nxt_ref):
        if nxt_ref is not None:
            scores(j + 1, nxt_ref)
        for r in range(tq // rb):
            rows = slice(r * rb, (r + 1) * rb)
            tiles = [cur_ref[rows, t * LANES:(t + 1) * LANES] for t in range(tk // LANES)]
            mx = functools.reduce(jnp.maximum, tiles)
            m_old = m_ref[rows, :]
            m_new = jnp.maximum(m_old, jnp.broadcast_to(jnp.max(mx, axis=-1, keepdims=True), m_old.shape))
            alpha_ref[rows, :] = jnp.exp2(m_old - m_new)
            for t, s_t in enumerate(tiles):
                p_ref[rows, t * LANES:(t + 1) * LANES] = jnp.exp2(s_t - m_new).astype(BF16)
            m_ref[rows, :] = m_new
        start = pl.multiple_of(j * tk, tk)
        pv = jnp.dot(p_ref[...], v_ref[pl.ds(start, tk), :], preferred_element_type=F32)
        alpha = alpha_ref[...]
        for t in range(acc_ref.shape[1] // LANES):
            cols = slice(t * LANES, (t + 1) * LANES)
            acc_ref[:, cols] = acc_ref[:, cols] * alpha + pv[:, cols]

    m_ref[...] = jnp.full(m_ref.shape, -jnp.inf, F32)
    acc_ref[...] = jnp.zeros(acc_ref.shape, F32)
    scores(0, s0_ref)

    def pair(i, carry):
        consume(2 * i, s0_ref, s1_ref)
        consume(2 * i + 1, s1_ref, s0_ref)
        return carry

    lax.fori_loop(0, nk // 2 - 1, pair, 0)
    consume(nk - 2, s0_ref, s1_ref)
    consume(nk - 1, s1_ref, None)
    acc = acc_ref[...]
    o_ref[...] = acc[:, :dv] / acc[:, dv:]


def _mla_attention(q, k, v, *, tq=1024, tk=1024, rb=64):
    b, h, s, dq = q.shape
    dv = MLA_V_HEAD
    assert (s // tk) % 2 == 0
    kern = functools.partial(_attn_kernel, tk=tk, rb=rb)
    return pl.pallas_call(
        kern,
        grid=(b, h, s // tq),
        in_specs=[
            pl.BlockSpec((None, None, tq, dq), lambda bi, hi, i: (bi, hi, i, 0)),
            pl.BlockSpec((None, None, s, dq), lambda bi, hi, i: (bi, hi, 0, 0)),
            pl.BlockSpec((None, None, s, 2 * dv), lambda bi, hi, i: (bi, hi, 0, 0)),
        ],
        out_specs=pl.BlockSpec((None, tq, dv), lambda bi, hi, i: (bi, i, hi)),
        out_shape=jax.ShapeDtypeStruct((b, s, h * dv), F32),
        scratch_shapes=[
            pltpu.VMEM((tq, tk), F32), pltpu.VMEM((tq, tk), F32), pltpu.VMEM((tq, tk), BF16),
            pltpu.VMEM((tq, 2 * dv), F32), pltpu.VMEM((tq, LANES), F32), pltpu.VMEM((tq, LANES), F32),
        ],
        compiler_params=_cparams("parallel", "parallel", "parallel"),
        name="mla_attn",
    )(q, k, v)


def _gla_chunk(q, k, v, g, tri, e3, hmask, vmask, *, reverse):
    c = q.shape[0]
    nsub = c // GLA_SUB
    beta = jnp.dot(tri, g, precision=HIGHEST, preferred_element_type=F32)
    tot = beta[0:1] if reverse else beta[c - 1:c]
    qd = q * jnp.exp(beta)
    kd = k * jnp.exp(tot - beta)
    ut = lax.dot_general(v.astype(BF16), kd.astype(BF16), (((0,), (0,)), ((), ())),
                         preferred_element_type=F32)
    dec = jnp.exp(tot)

    jj = lax.broadcasted_iota(jnp.int32, (GLA_SUB, GLA_SUB, LANES), 0)
    ii = lax.broadcasted_iota(jnp.int32, (GLA_SUB, GLA_SUB, LANES), 1)
    keep = (jj > ii) if reverse else (jj <= ii)

    v16 = v.astype(BF16)
    outs = []
    for sb in range(nsub):
        r0, r1 = sb * GLA_SUB, (sb + 1) * GLA_SUB
        qs, ks, bs, vs = q[r0:r1], k[r0:r1], beta[r0:r1], v[r0:r1]
        diff = jnp.where(keep, bs[None, :, :] - bs[:, None, :], NEG_BIG)
        p = (qs[None, :, :] * ks[:, None, :]) * jnp.exp(diff)
        w = jnp.dot(p.reshape(GLA_SUB * GLA_SUB, LANES).astype(BF16), e3, preferred_element_type=F32)
        o_sb = jnp.sum(w.reshape(GLA_SUB, GLA_SUB, w.shape[-1]) * vs[:, None, :], axis=0)
        if reverse:
            j0, j1 = r1, c
            ref = beta[r1:r1 + 1] if r1 < c else None
        else:
            j0, j1 = 0, r0
            ref = beta[r0 - 1:r0] if r0 > 0 else None
        if ref is not None:
            qt = qs * jnp.exp(bs - ref)
            kt = (k[j0:j1] * jnp.exp(ref - beta[j0:j1])).astype(BF16)
            q_stack = jnp.concatenate([qt * hmask[hh:hh + 1] for hh in range(GLA_HEADS)], axis=0).astype(BF16)
            a = lax.dot_general(q_stack, kt, (((1,), (1,)), ((), ())), preferred_element_type=F32)
            o_all = jnp.dot(a.astype(BF16), v16[j0:j1], preferred_element_type=F32)
            for hh in range(GLA_HEADS):
                o_sb = o_sb + o_all[hh * GLA_SUB:(hh + 1) * GLA_SUB] * vmask[hh:hh + 1]
        outs.append(o_sb)
    intra = jnp.concatenate(outs, axis=0)
    return intra, qd, ut, dec


def _gla_kernel(qf_ref, kf_ref, vf_ref, gf_ref, qb_ref, kb_ref, vb_ref, gb_ref,
                tril_ref, triu_ref, e3_ref, hmask_ref, vmask_ref, stmask_ref,
                of_ref, ob_ref, stf_ref, stb_ref):
    @pl.when(pl.program_id(1) == 0)
    def _():
        stf_ref[...] = jnp.zeros_like(stf_ref)
        stb_ref[...] = jnp.zeros_like(stb_ref)

    tb = qf_ref.shape[0]
    nc = tb // GLA_CHUNK
    e3 = e3_ref[...]
    hmask = hmask_ref[...]
    vmask = vmask_ref[...]
    stmask = stmask_ref[...]

    def step(c, carry):
        for reverse in (False, True):
            if reverse:
                q_ref, k_ref, v_ref, g_ref, o_ref, st_ref, tri = (
                    qb_ref, kb_ref, vb_ref, gb_ref, ob_ref, stb_ref, triu_ref[...])
                r0 = pl.multiple_of((nc - 1 - c) * GLA_CHUNK, GLA_CHUNK)
            else:
                q_ref, k_ref, v_ref, g_ref, o_ref, st_ref, tri = (
                    qf_ref, kf_ref, vf_ref, gf_ref, of_ref, stf_ref, tril_ref[...])
                r0 = pl.multiple_of(c * GLA_CHUNK, GLA_CHUNK)
            rows = pl.ds(r0, GLA_CHUNK)
            intra, qd, ut, dec = _gla_chunk(q_ref[rows, :], k_ref[rows, :], v_ref[rows, :], g_ref[rows, :],
                                            tri, e3, hmask, vmask, reverse=reverse)
            st = st_ref[...]
            inter = lax.dot_general(qd.astype(BF16), st.astype(BF16), (((1,), (1,)), ((), ())),
                                    preferred_element_type=F32)
            o_ref[rows, :] = intra + inter
            st_ref[...] = st * dec + ut * stmask
        return carry

    lax.fori_loop(0, nc, step, 0)


def _gla(gq, gk, gv, gf, gb, *, tb=512):
    b, s, kw = gq.shape
    vw = gv.shape[-1]
    nb = s // tb
    dk = kw // GLA_HEADS
    dv = vw // GLA_HEADS
    idx = np.arange(GLA_CHUNK)
    tril = jnp.asarray((idx[None, :] <= idx[:, None]).astype(np.float32))
    triu = jnp.asarray((idx[None, :] >= idx[:, None]).astype(np.float32))
    d_head = np.arange(kw) // dk
    e_head = np.arange(vw) // dv
    e3 = jnp.asarray((d_head[:, None] == e_head[None, :]).astype(np.float32)).astype(BF16)
    hmask = jnp.asarray((np.arange(GLA_HEADS)[:, None] == d_head[None, :]).astype(np.float32))
    vmask = jnp.asarray((np.arange(GLA_HEADS)[:, None] == e_head[None, :]).astype(np.float32))
    stmask = jnp.asarray((e_head[:, None] == d_head[None, :]).astype(np.float32))

    fwd = lambda w: pl.BlockSpec((None, tb, w), lambda bi, i: (bi, i, 0))
    bwd = lambda w: pl.BlockSpec((None, tb, w), lambda bi, i: (bi, nb - 1 - i, 0))
    return pl.pallas_call(
        _gla_kernel,
        grid=(b, nb),
        in_specs=[
            fwd(kw), fwd(kw), fwd(vw), fwd(kw),
            bwd(kw), bwd(kw), bwd(vw), bwd(kw),
            _const_spec((GLA_CHUNK, GLA_CHUNK)), _const_spec((GLA_CHUNK, GLA_CHUNK)),
            _const_spec((kw, vw)), _const_spec((GLA_HEADS, kw)), _const_spec((GLA_HEADS, vw)),
            _const_spec((vw, kw)),
        ],
        out_specs=[fwd(vw), bwd(vw)],
        out_shape=[jax.ShapeDtypeStruct((b, s, vw), F32), jax.ShapeDtypeStruct((b, s, vw), F32)],
        scratch_shapes=[pltpu.VMEM((vw, kw), F32), pltpu.VMEM((vw, kw), F32)],
        compiler_params=_cparams("parallel", "arbitrary"),
        name="gla",
    )(gq, gk, gv, gf, gq, gk, gv, gb, tril, triu, e3, hmask, vmask, stmask)


def _short_conv_kernel(x_ref, w_ref, b_ref, o_ref):
    x = x_ref[...]
    n = x.shape[0]
    row = lax.broadcasted_iota(jnp.int32, x.shape, 0)
    prev = jnp.where(row == 0, 0.0, pltpu.roll(x, 1, axis=0))
    nxt = jnp.where(row == n - 1, 0.0, pltpu.roll(x, n - 1, axis=0))
    w = w_ref[...]
    o_ref[...] = b_ref[...] + prev * w[0:1] + x * w[1:2] + nxt * w[2:3]


def _short_conv(hy_in, conv_w, conv_b):
    b, s, c3 = hy_in.shape
    c = c3 // 3
    nt = c // LANES
    return pl.pallas_call(
        _short_conv_kernel,
        grid=(b, c3 // LANES),
        in_specs=[
            pl.BlockSpec((None, s, LANES), lambda bi, j: (bi, 0, j)),
            pl.BlockSpec((HY_SHORT, LANES), lambda bi, j: (0, j)),
            pl.BlockSpec((1, LANES), lambda bi, j: (0, j)),
        ],
        out_specs=pl.BlockSpec((None, None, s, LANES), lambda bi, j: (j // nt, bi, 0, j % nt)),
        out_shape=jax.ShapeDtypeStruct((3, b, s, c), F32),
        compiler_params=_cparams("parallel", "parallel"),
        name="hy_short_conv",
    )(hy_in, conv_w, conv_b.reshape(1, c3))


def _filter_kernel(band_ref, w1_ref, b1_ref, f1_ref, w2_ref, b2_ref, f2_ref, w3_ref, delta_ref, o_ref,
                   *, seq):
    tr = o_ref.shape[1]
    c = delta_ref.shape[-1]
    n = pl.program_id(0) * tr + lax.broadcasted_iota(jnp.int32, (tr, LANES), 0)
    lane = lax.broadcasted_iota(jnp.int32, (tr, LANES), 1)
    pos = jnp.where(n < seq, n, 2 * seq - n).astype(F32)
    t = pos / (seq - 1)
    ang = (2.0 * math.pi * pos / seq) * band_ref[...]
    z = jnp.where(lane == 0, t,
                  jnp.where(lane <= HY_BANDS, jnp.cos(ang),
                            jnp.where(lane <= 2 * HY_BANDS, -jnp.sin(ang), 0.0)))
    h = jnp.sin(f1_ref[...] * (jnp.dot(z, w1_ref[...], precision=HIGHEST, preferred_element_type=F32)
                               + b1_ref[...]))
    h = jnp.sin(f2_ref[...] * (jnp.dot(h, w2_ref[...], precision=HIGHEST, preferred_element_type=F32)
                               + b2_ref[...]))
    h = jnp.dot(h, w3_ref[...], precision=HIGHEST, preferred_element_type=F32)
    t_c = jnp.concatenate([t] * (c // LANES), axis=-1)
    window = jnp.exp(-t_c * delta_ref[...]) + HY_DECAY_SHIFT
    live = jnp.concatenate([n] * (c // LANES), axis=-1) != seq
    for o in range(HY_ORDER):
        o_ref[o] = jnp.where(live, h[:, o * c:(o + 1) * c] * window, 0.0)


def _hyena_filters(seq, w1, b1, f1, w2, b2, f2, w3, *, tr=512):
    emb, hid = w1.shape
    c = w3.shape[1] // (HY_ORDER * 2)
    n = 2 * seq
    bands = np.zeros((1, LANES), np.float32)
    bvals = np.linspace(1e-4, HY_BANDS - 1, HY_BANDS, dtype=np.float32)
    bands[0, 1:1 + HY_BANDS] = bvals
    bands[0, 1 + HY_BANDS:1 + 2 * HY_BANDS] = bvals
    max_decay = math.log(HY_DECAY_TARGET) / HY_FAST_DECAY_PCT
    min_decay = math.log(HY_DECAY_TARGET) / HY_SLOW_DECAY_PCT
    deltas = np.abs(np.linspace(min_decay, max_decay, c, dtype=np.float32)).reshape(1, c)
    w1p = jnp.pad(w1, ((0, LANES - emb), (0, 0)))
    w3d = w3.reshape(hid, HY_ORDER, 2, c).transpose(2, 0, 1, 3).reshape(2, hid, HY_ORDER * c)
    half_tiles = seq // tr
    kern = functools.partial(_filter_kernel, seq=seq)
    return pl.pallas_call(
        kern,
        grid=(n // tr,),
        in_specs=[
            _const_spec((1, LANES)),
            _const_spec((LANES, hid)), _const_spec((1, hid)), _const_spec((1, hid)),
            _const_spec((hid, hid)), _const_spec((1, hid)), _const_spec((1, hid)),
            pl.BlockSpec((None, hid, HY_ORDER * c), lambda i: (i // half_tiles, 0, 0)),
            _const_spec((1, c)),
        ],
        out_specs=pl.BlockSpec((HY_ORDER, tr, c), lambda i: (0, i, 0)),
        out_shape=jax.ShapeDtypeStruct((HY_ORDER, n, c), F32),
        compiler_params=_cparams("parallel"),
        name="hy_filter",
    )(jnp.asarray(bands), w1p, b1.reshape(1, hid), f1.reshape(1, hid), w2, b2.reshape(1, hid),
      f2.reshape(1, hid), w3d, jnp.asarray(deltas))


def _dft_tables(p):
    idx = np.arange(p)
    ang = 2.0 * np.pi * np.outer(idx, idx) / p
    fr, fi = np.cos(ang), -np.sin(ang)
    tang = 2.0 * np.pi * np.outer(idx, idx) / (p * p)
    twr, twi = np.cos(tang), -np.sin(tang)
    return fr, fi, twr, twi


def _split_bf16(x):
    hi = x.astype(BF16)
    return hi, (x - hi.astype(F32)).astype(BF16)


def _dft_dot(mat, data, *, split):
    dot = functools.partial(jnp.dot, preferred_element_type=F32)
    if not split:
        return dot(mat.astype(BF16), data.astype(BF16))
    mh, ml = _split_bf16(mat)
    dh, dl = _split_bf16(data)
    return dot(mh, dh) + (dot(mh, dl) + dot(ml, dh))


def _stage_a_kernel(x_ref, m_ref, o_ref, *, split):
    x = jnp.concatenate([x_ref[0], x_ref[1]], axis=0)
    o_ref[0] = _dft_dot(m_ref[...], x, split=split).astype(o_ref.dtype)


def _stage_a(x4, idx, mat, *, split, ct=2048):
    _, g2, hp, cols = x4.shape
    rows = mat.shape[0]
    return pl.pallas_call(
        functools.partial(_stage_a_kernel, split=split),
        grid=(g2 // 2, cols // ct),
        in_specs=[
            pl.BlockSpec((None, 2, hp, ct), lambda g, j: (idx, g, 0, j)),
            _const_spec(mat.shape),
        ],
        out_specs=pl.BlockSpec((1, rows, ct), lambda g, j: (g, 0, j)),
        out_shape=jax.ShapeDtypeStruct((g2 // 2, rows, cols), F32 if split else BF16),
        compiler_params=_cparams("parallel", "parallel"),
        name="hy_dft_a",
    )(x4, mat)


def _fwd_matrix(fr, fi, twr, twi):
    gr = fr * twr - fi * twi
    gi = fr * twi + fi * twr
    return jnp.concatenate([jnp.concatenate([gr, -gi], axis=1), jnp.concatenate([gi, gr], axis=1)], axis=0)


def _stack_pairs(a_ref, j):
    g = a_ref.shape[0]
    re = jnp.concatenate([a_ref[i, 0, j] for i in range(g)], axis=-1)
    im = jnp.concatenate([a_ref[i, 1, j] for i in range(g)], axis=-1)
    return jnp.concatenate([re, im], axis=0)


def _stage_c_kernel(a_ref, fr_ref, fi_ref, twr_ref, twi_ref, o_ref):
    g, _, kk, p, c = a_ref.shape
    fr, fi = fr_ref[...], fi_ref[...]
    for j in range(kk):
        gs = _fwd_matrix(fr, fi, twr_ref[j], twi_ref[j])
        x = _dft_dot(gs, _stack_pairs(a_ref, j), split=True)
        for i in range(g):
            o_ref[i, 0, j] = x[:p, i * c:(i + 1) * c]
            o_ref[i, 1, j] = x[p:, i * c:(i + 1) * c]


def _conv_core_kernel(a_ref, kh_ref, fr_ref, fi_ref, twr_ref, twi_ref, twbr_ref, twbi_ref, o_ref):
    g, _, kk, p, c = a_ref.shape
    fr, fi = fr_ref[...], fi_ref[...]
    for j in range(kk):
        gs = _fwd_matrix(fr, fi, twr_ref[j], twi_ref[j])
        x = _dft_dot(gs, _stack_pairs(a_ref, j), split=False)
        xr, xi = x[:p], x[p:]
        kr = jnp.concatenate([kh_ref[0, j]] * g, axis=-1)
        ki = jnp.concatenate([kh_ref[1, j]] * g, axis=-1)
        y = jnp.concatenate([xr * kr - xi * ki, xr * ki + xi * kr], axis=0)
        br = fr * twbr_ref[j] - fi * twbi_ref[j]
        bi = -(fr * twbi_ref[j] + fi * twbr_ref[j])
        bs = jnp.concatenate([jnp.concatenate([br, -bi], axis=1), jnp.concatenate([bi, br], axis=1)], axis=0)
        r = _dft_dot(bs, y, split=False).astype(o_ref.dtype)
        for i in range(g):
            o_ref[i, 0, j] = r[:p, i * c:(i + 1) * c]
            o_ref[i, 1, j] = r[p:, i * c:(i + 1) * c]


def _tw_specs(p, kk):
    row = pl.BlockSpec((kk, 1, p), lambda i: (i, 0, 0))
    return row


def _stage_c(a5, tabs, *, kk=4):
    g, _, p, _, c = a5.shape
    fr, fi, twr, twi = tabs
    blk = pl.BlockSpec((g, 2, kk, p, c), lambda i: (0, 0, i, 0, 0))
    return pl.pallas_call(
        _stage_c_kernel,
        grid=(p // kk,),
        in_specs=[blk, _const_spec((p, p)), _const_spec((p, p)), _tw_specs(p, kk), _tw_specs(p, kk)],
        out_specs=blk,
        out_shape=jax.ShapeDtypeStruct(a5.shape, F32),
        compiler_params=_cparams("parallel"),
        name="hy_dft_c",
    )(a5, fr, fi, twr.reshape(p, 1, p), twi.reshape(p, 1, p))


def _conv_core(a5, khat, tabs, twb, *, kk=8):
    g, _, p, _, c = a5.shape
    fr, fi, twr, twi = tabs
    twbr, twbi = twb
    blk = pl.BlockSpec((g, 2, kk, p, c), lambda i: (0, 0, i, 0, 0))
    return pl.pallas_call(
        _conv_core_kernel,
        grid=(p // kk,),
        in_specs=[
            blk,
            pl.BlockSpec((2, kk, p, c), lambda i: (0, i, 0, 0)),
            _const_spec((p, p)), _const_spec((p, p)), _tw_specs(p, kk), _tw_specs(p, kk),
            pl.BlockSpec((kk, p, p), lambda i: (i, 0, 0)),
            pl.BlockSpec((kk, p, p), lambda i: (i, 0, 0)),
        ],
        out_specs=blk,
        out_shape=jax.ShapeDtypeStruct(a5.shape, BF16),
        compiler_params=_cparams("parallel"),
        name="hy_conv_core",
    )(a5, khat, fr, fi, twr.reshape(p, 1, p), twi.reshape(p, 1, p), twbr, twbi)


def _stage_b_kernel(r_ref, m_ref, u_ref, gate_ref, skip_ref, o_ref):
    y = _dft_dot(m_ref[...], r_ref[0], split=False)
    hp = u_ref.shape[1]
    skip = skip_ref[...]
    o_ref[0] = gate_ref[0] * (y[:hp] + skip * u_ref[0])
    o_ref[1] = gate_ref[1] * (y[hp:] + skip * u_ref[1])


def _stage_b(r3, mat, u4, u_idx, gate4, gate_idx, skip_row, *, ct=2048):
    g, rows, cols = r3.shape
    hp = mat.shape[0] // 2
    nb = u4.shape[1]
    return pl.pallas_call(
        _stage_b_kernel,
        grid=(g, cols // ct),
        in_specs=[
            pl.BlockSpec((1, rows, ct), lambda gi, j: (gi, 0, j)),
            _const_spec(mat.shape),
            pl.BlockSpec((None, 2, hp, ct), lambda gi, j: (u_idx, gi, 0, j)),
            pl.BlockSpec((None, 2, hp, ct), lambda gi, j: (gate_idx, gi, 0, j)),
            pl.BlockSpec((1, ct), lambda gi, j: (0, j)),
        ],
        out_specs=pl.BlockSpec((2, hp, ct), lambda gi, j: (gi, 0, j)),
        out_shape=jax.ShapeDtypeStruct((nb, hp, cols), F32),
        compiler_params=_cparams("parallel", "parallel"),
        name="hy_dft_b",
    )(r3, mat, u4, gate4, skip_row)


def _hyena(hy_in, conv_w, conv_b, w1, b1, f1, w2, b2, f2, w3, skip):
    b, s, c3 = hy_in.shape
    c = c3 // 3
    p = math.isqrt(2 * s)
    assert p * p == 2 * s and b % 2 == 0
    hp = p // 2
    cols = p * c
    fr, fi, twr, twi = _dft_tables(p)
    scale = 1.0 / (p * p)
    mat_a = np.block([[fr[:, :hp], -fi[:, :hp]], [fi[:, :hp], fr[:, :hp]]]).astype(np.float32)
    mat_k = np.concatenate([fr, fi], axis=0).astype(np.float32)
    br, bi = fr[:hp] * scale, -fi[:hp] * scale
    mat_b = np.block([[br, -bi], [bi, br]]).astype(np.float32)
    tabs = tuple(jnp.asarray(t.astype(np.float32)) for t in (fr, fi, twr, twi))
    twb = tuple(jnp.broadcast_to(t[:, :, None], (p, p, p)) for t in tabs[2:])

    uvx = _short_conv(hy_in, conv_w, conv_b).reshape(3, b, hp, cols)
    kern = _hyena_filters(s, w1, b1, f1, w2, b2, f2, w3)
    ka = _stage_a(kern.reshape(1, HY_ORDER * 2, hp, cols), 0, jnp.asarray(mat_k), split=True)
    khat = _stage_c(ka.reshape(HY_ORDER, 2, p, p, c), tabs)

    skip_rows = jnp.tile(skip, (1, p))
    g = b // 2

    def long_conv(u4, u_idx, gate_idx, order):
        a = _stage_a(u4, u_idx, jnp.asarray(mat_a), split=False)
        r = _conv_core(a.reshape(g, 2, p, p, c), khat[order], tabs, twb)
        return _stage_b(r.reshape(g, 2 * p, cols), jnp.asarray(mat_b), u4, u_idx, uvx, gate_idx,
                        skip_rows[order:order + 1])

    z = long_conv(uvx, 0, 1, 0)
    y = long_conv(z[None], 0, 2, 1)
    return y.reshape(b, s, c)


def _out_proj_kernel(x_ref, hy_ref, mla_ref, of_ref, ob_ref, r_ref, ghy_ref, gmla_ref, ggla_ref,
                     seg_ref, w_ref, o_ref):
    y_hy = _rms(hy_ref[...], ghy_ref[...])
    y_mla = _rms(mla_ref[...], gmla_ref[...])
    o = of_ref[...] + ob_ref[...]
    ms = jnp.dot(o * o, seg_ref[...], precision=HIGHEST, preferred_element_type=F32)
    r = r_ref[...]
    y_gla = o * lax.rsqrt(ms + NORM_EPS) * ggla_ref[...] * (r * jax.nn.sigmoid(r))
    cat = jnp.concatenate([y_hy, y_mla, y_gla], axis=-1).astype(BF16)
    o_ref[...] = x_ref[...] + jnp.dot(cat, w_ref[...], preferred_element_type=F32)


def _out_proj(x2d, y_hy, o_mla, o_f, o_b, r, g_hy, g_mla, g_head, w_out, *, tm=512):
    t, d = x2d.shape
    c_hy, c_mla, c_gla = y_hy.shape[1], o_mla.shape[1], o_f.shape[1]
    dv = c_gla // GLA_HEADS
    head = np.arange(c_gla) // dv
    seg = jnp.asarray((head[:, None] == head[None, :]).astype(np.float32) / dv)
    tok = lambda w: pl.BlockSpec((tm, w), lambda i: (i, 0))
    return pl.pallas_call(
        _out_proj_kernel,
        grid=(t // tm,),
        in_specs=[
            tok(d), tok(c_hy), tok(c_mla), tok(c_gla), tok(c_gla), tok(c_gla),
            _const_spec((1, c_hy)), _const_spec((1, c_mla)), _const_spec((1, c_gla)),
            _const_spec((c_gla, c_gla)), _const_spec(w_out.shape),
        ],
        out_specs=tok(d),
        out_shape=jax.ShapeDtypeStruct((t, d), F32),
        compiler_params=_cparams("parallel"),
        name="out_proj",
    )(x2d, y_hy, o_mla, o_f, o_b, r, g_hy.reshape(1, -1), g_mla.reshape(1, -1),
      jnp.tile(g_head, GLA_HEADS).reshape(1, -1), seg, w_out.astype(BF16))


def kernel(x, ffn1_norm, ffn1_w_gate, ffn1_w_up, ffn1_w_down, mix_norm, w_in, hy_conv_w, hy_conv_b, hy_filt_w1, hy_filt_b1, hy_filt_freq1, hy_filt_w2, hy_filt_b2, hy_filt_freq2, hy_filt_w3, hy_skip, hy_out_norm, mla_q_norm, mla_w_uq, mla_kv_norm, mla_w_ukv, mla_out_norm, gla_w_gate_fwd, gla_b_gate_fwd, gla_w_gate_bwd, gla_b_gate_bwd, gla_head_norm, w_out, ffn2_norm, ffn2_w_gate, ffn2_w_up, ffn2_w_down, final_norm):
    b, s, d = x.shape
    depth = ffn1_norm.shape[0]
    t = b * s
    for l in range(depth):
        x2 = _ffn(x.reshape(t, d), ffn1_norm[l], ffn1_w_gate[l], ffn1_w_up[l], ffn1_w_down[l], final_norm,
                  final=False)
        (hy_in, q, k, v, gq, gk, gv, gf, gb, gr) = _in_proj(
            x2.reshape(b, s, d), mix_norm[l], w_in[l], mla_q_norm[l], mla_w_uq[l], mla_kv_norm[l],
            mla_w_ukv[l], gla_w_gate_fwd[l], gla_b_gate_fwd[l], gla_w_gate_bwd[l], gla_b_gate_bwd[l])
        y_hy = _hyena(hy_in, hy_conv_w[l], hy_conv_b[l], hy_filt_w1[l], hy_filt_b1[l], hy_filt_freq1[l],
                      hy_filt_w2[l], hy_filt_b2[l], hy_filt_freq2[l], hy_filt_w3[l], hy_skip[l])
        o_mla = _mla_attention(q, k, v)
        o_f, o_b = _gla(gq, gk, gv, gf, gb)
        x3 = _out_proj(x2, y_hy.reshape(t, -1), o_mla.reshape(t, -1), o_f.reshape(t, -1),
                       o_b.reshape(t, -1), gr.reshape(t, -1), hy_out_norm[l], mla_out_norm[l],
                       gla_head_norm[l], w_out[l])
        x = _ffn(x3, ffn2_norm[l], ffn2_w_gate[l], ffn2_w_up[l], ffn2_w_down[l], final_norm,
                 final=(l == depth - 1)).reshape(b, s, d)
    return x
```

```python
import functools
import math

import jax
import jax.numpy as jnp
import numpy as np
from jax import lax
from jax.experimental import pallas as pl
from jax.experimental.pallas import tpu as pltpu

F32 = jnp.float32
BF16 = jnp.bfloat16
HIGHEST = lax.Precision.HIGHEST

NORM_EPS = 1e-6
HY_ORDER = 2
HY_SHORT = 3
HY_BANDS = 16
HY_FAST_DECAY_PCT = 0.3
HY_SLOW_DECAY_PCT = 1.5
HY_DECAY_TARGET = 1e-2
HY_DECAY_SHIFT = 0.0
MLA_HEADS = 4
MLA_QK_NOPE = 128
MLA_QK_ROPE = 64
MLA_V_HEAD = 128
ROPE_BASE = 10000.0
GLA_HEADS = 4
GLA_GATE_RANK = 16
GLA_GATE_NORM = 16.0

LANES = 128
GLA_CHUNK = 64
GLA_SUB = 16
VMEM_LIMIT = 56 * 1024 * 1024

NEG_BIG = -1e30


def _cparams(*sem):
    return pltpu.CompilerParams(dimension_semantics=sem, vmem_limit_bytes=VMEM_LIMIT)


def _rms(x, g):
    ms = jnp.mean(x * x, axis=-1, keepdims=True)
    return x * lax.rsqrt(ms + NORM_EPS) * g


def _const_spec(shape):
    nd = len(shape)
    return pl.BlockSpec(shape, lambda *_: (0,) * nd)


def _ffn_kernel(x_ref, g_ref, wg_ref, wu_ref, wd_ref, fg_ref, o_ref, *, ff_chunk, final):
    x = x_ref[...]
    xn = _rms(x, g_ref[...]).astype(BF16)
    d_ff = wg_ref.shape[1]
    acc = jnp.zeros_like(x)
    for c in range(0, d_ff, ff_chunk):
        gate = jnp.dot(xn, wg_ref[:, c:c + ff_chunk], preferred_element_type=F32)
        up = jnp.dot(xn, wu_ref[:, c:c + ff_chunk], preferred_element_type=F32)
        h = (gate * jax.nn.sigmoid(gate) * up).astype(BF16)
        acc = acc + jnp.dot(h, wd_ref[c:c + ff_chunk, :], preferred_element_type=F32)
    y = x + 0.5 * acc
    if final:
        y = _rms(y, fg_ref[...])
    o_ref[...] = y


def _ffn(x2d, norm_g, w_gate, w_up, w_down, final_g, *, final, tm=512, ff_chunk=256):
    t, d = x2d.shape
    d_ff = w_gate.shape[1]
    kern = functools.partial(_ffn_kernel, ff_chunk=ff_chunk, final=final)
    return pl.pallas_call(
        kern,
        grid=(t // tm,),
        in_specs=[
            pl.BlockSpec((tm, d), lambda i: (i, 0)),
            _const_spec((1, d)),
            _const_spec((d, d_ff)),
            _const_spec((d, d_ff)),
            _const_spec((d_ff, d)),
            _const_spec((1, d)),
        ],
        out_specs=pl.BlockSpec((tm, d), lambda i: (i, 0)),
        out_shape=jax.ShapeDtypeStruct((t, d), F32),
        compiler_params=_cparams("parallel"),
        name="ffn",
    )(x2d, norm_g.reshape(1, d), w_gate.astype(BF16), w_up.astype(BF16), w_down.astype(BF16),
      final_g.reshape(1, d))


def _log_sigmoid(z):
    return jnp.minimum(z, 0.0) - jnp.log(1.0 + jnp.exp(-jnp.abs(z)))


def _in_proj_kernel(x_ref, g_ref, win_ref, qn_ref, wuq_ref, kvn_ref, wukv_ref, cos_ref, sin_ref,
                    wgate_ref, bgate_ref,
                    hy_ref, q_ref, k_ref, v_ref, gq_ref, gk_ref, gv_ref, gf_ref, gb_ref, gr_ref,
                    *, off, q_scale, gla_q_scale):
    x = x_ref[0]
    xn = _rms(x, g_ref[...]).astype(BF16)
    proj = jnp.dot(xn, win_ref[...], preferred_element_type=F32)

    def blk(name):
        lo, hi = off[name]
        return proj[:, lo:hi]

    hy_ref[0] = blk("hy")

    qn = _rms(blk("cq"), qn_ref[...]).astype(BF16)
    qf = jnp.dot(qn, wuq_ref[...], preferred_element_type=F32)
    kvn = _rms(blk("ckv"), kvn_ref[...]).astype(BF16)
    kvf = jnp.dot(kvn, wukv_ref[...], preferred_element_type=F32)
    cos = cos_ref[...]
    sin = sin_ref[...]
    k_rope = blk("kr") * cos + blk("kr_rot") * sin
    ones = jnp.ones_like(k_rope)
    qw = 3 * LANES
    kvw = 2 * LANES
    for h in range(MLA_HEADS):
        q_nope = qf[:, h * qw:h * qw + LANES]
        q_rope = (qf[:, h * qw + LANES:h * qw + 2 * LANES] * cos
                  + qf[:, h * qw + 2 * LANES:(h + 1) * qw] * sin)
        q_ref[0, h] = (jnp.concatenate([q_nope, q_rope], axis=-1) * q_scale).astype(BF16)
        k_ref[0, h] = jnp.concatenate([kvf[:, h * kvw:h * kvw + LANES], k_rope], axis=-1).astype(BF16)
        v_ref[0, h] = jnp.concatenate([kvf[:, h * kvw + LANES:(h + 1) * kvw], ones], axis=-1).astype(BF16)

    gq_ref[0] = blk("gq") * gla_q_scale
    gk_ref[0] = blk("gk")
    gv_ref[0] = blk("gv")
    z = jnp.dot(blk("gate"), wgate_ref[...], precision=HIGHEST, preferred_element_type=F32) + bgate_ref[...]
    gates = _log_sigmoid(z) / GLA_GATE_NORM
    gk_w = gf_ref.shape[-1]
    gf_ref[0] = gates[:, :gk_w]
    gb_ref[0] = gates[:, gk_w:]
    gr_ref[0] = blk("gr")


def _rope_rot_cols(w):
    half = w.shape[-1] // 2
    return jnp.concatenate([-w[..., half:], w[..., :half]], axis=-1)


def _pad_cols(w, width):
    return jnp.pad(w, ((0, 0), (0, width - w.shape[-1])))


def _in_proj(x, mix_g, w_in, q_norm, w_uq, kv_norm, w_ukv, w_gf, b_gf, w_gb, b_gb, *, tm=512):
    b, s, d = x.shape
    hy_in = 3 * (d // 4)
    q_lora = d // 4
    kv_lora = d // 8
    gla_w = d // 4
    gla_key = gla_w // 2
    rope = MLA_QK_ROPE
    splits = (hy_in, q_lora, kv_lora, rope, gla_key, gla_key, gla_w, GLA_GATE_RANK, GLA_GATE_RANK, gla_w)
    offs = np.concatenate([[0], np.cumsum(splits)])
    (w_hy, w_cq, w_ckv, w_kr, w_gq, w_gk, w_gv, w_gfl, w_gbl, w_gr) = [
        w_in[:, offs[i]:offs[i + 1]] for i in range(len(splits))]
    pieces = [
        ("hy", w_hy), ("cq", w_cq), ("ckv", w_ckv),
        ("kr", _pad_cols(w_kr, LANES)), ("kr_rot", _pad_cols(_rope_rot_cols(w_kr), LANES)),
        ("gq", w_gq), ("gk", w_gk), ("gv", w_gv),
        ("gate", _pad_cols(jnp.concatenate([w_gfl, w_gbl], axis=1), LANES)),
        ("gr", w_gr),
    ]
    off = {}
    pos = 0
    for name, w in pieces:
        off[name] = (pos, pos + w.shape[1])
        pos += w.shape[1]
    win_r = jnp.concatenate([w for _, w in pieces], axis=1).astype(BF16)
    n_in = win_r.shape[1]

    dqk = MLA_QK_NOPE + MLA_QK_ROPE
    wq = w_uq.reshape(q_lora, MLA_HEADS, dqk)
    wq_nope = wq[..., :MLA_QK_NOPE]
    wq_rope = wq[..., MLA_QK_NOPE:]
    zpad = jnp.zeros((q_lora, MLA_HEADS, LANES - rope), F32)
    wuq_r = jnp.concatenate([wq_nope, wq_rope, zpad, _rope_rot_cols(wq_rope), zpad], axis=-1)
    wuq_r = wuq_r.reshape(q_lora, MLA_HEADS * 3 * LANES).astype(BF16)
    wukv_r = w_ukv.astype(BF16)

    half = rope // 2
    inv = ROPE_BASE ** (-jnp.arange(half, dtype=F32) * 2.0 / rope)
    ang = jnp.arange(s, dtype=F32)[:, None] * inv[None, :]
    zt = jnp.zeros((s, LANES - rope), F32)
    cos_t = jnp.concatenate([jnp.cos(ang), jnp.cos(ang), zt], axis=1)
    sin_t = jnp.concatenate([jnp.sin(ang), jnp.sin(ang), zt], axis=1)

    wgate = jnp.zeros((LANES, 2 * gla_key), F32)
    wgate = wgate.at[:GLA_GATE_RANK, :gla_key].set(w_gf)
    wgate = wgate.at[GLA_GATE_RANK:2 * GLA_GATE_RANK, gla_key:].set(w_gb)
    bgate = jnp.concatenate([b_gf, b_gb]).reshape(1, 2 * gla_key)

    kern = functools.partial(_in_proj_kernel, off=off, q_scale=float(dqk) ** -0.5 * math.log2(math.e),
                             gla_q_scale=float(gla_key // GLA_HEADS) ** -0.5)
    tok = lambda w: pl.BlockSpec((1, tm, w), lambda bi, i: (bi, i, 0))
    head = lambda w: pl.BlockSpec((1, MLA_HEADS, tm, w), lambda bi, i: (bi, 0, i, 0))
    outs = pl.pallas_call(
        kern,
        grid=(b, s // tm),
        in_specs=[
            tok(d),
            _const_spec((1, d)),
            _const_spec((d, n_in)),
            _const_spec((1, q_lora)),
            _const_spec((q_lora, MLA_HEADS * 3 * LANES)),
            _const_spec((1, kv_lora)),
            _const_spec((kv_lora, MLA_HEADS * 2 * LANES)),
            pl.BlockSpec((tm, LANES), lambda bi, i: (i, 0)),
            pl.BlockSpec((tm, LANES), lambda bi, i: (i, 0)),
            _const_spec((LANES, 2 * gla_key)),
            _const_spec((1, 2 * gla_key)),
        ],
        out_specs=[
            tok(hy_in),
            head(2 * LANES), head(2 * LANES), head(2 * LANES),
            tok(gla_key), tok(gla_key), tok(gla_w), tok(gla_key), tok(gla_key), tok(gla_w),
        ],
        out_shape=[
            jax.ShapeDtypeStruct((b, s, hy_in), F32),
            jax.ShapeDtypeStruct((b, MLA_HEADS, s, 2 * LANES), BF16),
            jax.ShapeDtypeStruct((b, MLA_HEADS, s, 2 * LANES), BF16),
            jax.ShapeDtypeStruct((b, MLA_HEADS, s, 2 * LANES), BF16),
            jax.ShapeDtypeStruct((b, s, gla_key), F32),
            jax.ShapeDtypeStruct((b, s, gla_key), F32),
            jax.ShapeDtypeStruct((b, s, gla_w), F32),
            jax.ShapeDtypeStruct((b, s, gla_key), F32),
            jax.ShapeDtypeStruct((b, s, gla_key), F32),
            jax.ShapeDtypeStruct((b, s, gla_w), F32),
        ],
        compiler_params=_cparams("parallel", "parallel"),
        name="in_proj",
    )(x, mix_g.reshape(1, d), win_r, q_norm.reshape(1, -1), wuq_r, kv_norm.reshape(1, -1), wukv_r,
      cos_t, sin_t, wgate, bgate)
    return outs


def _attn_kernel(q_ref, k_ref, v_ref, o_ref, s0_ref, s1_ref, p_ref, acc_ref, m_ref, alpha_ref, *, tk, rb):
    q = q_ref[...]
    tq = q.shape[0]
    nk = k_ref.shape[0] // tk
    dv = o_ref.shape[-1]

    def scores(j, dst_ref):
        start = pl.multiple_of(j * tk, tk)
        dst_ref[...] = lax.dot_general(q, k_ref[pl.ds(start, tk), :], (((1,), (1,)), ((), ())),
                                       preferred_element_type=F32)

    def consume(j, cur_ref, nxt_ref):
        if nxt_ref is not None:
            scores(j + 1, nxt_ref)
        for r in range(tq // rb):
            rows = slice(r * rb, (r + 1) * rb)
            tiles = [cur_ref[rows, t * LANES:(t + 1) * LANES] for t in range(tk // LANES)]
            mx = functools.reduce(jnp.maximum, tiles)
            m_old = m_ref[rows, :]
            m_new = jnp.maximum(m_old, jnp.broadcast_to(jnp.max(mx, axis=-1, keepdims=True), m_old.shape))
            alpha_ref[rows, :] = jnp.exp2(m_old - m_new)
            for t, s_t in enumerate(tiles):
                p_ref[rows, t * LANES:(t + 1) * LANES] = jnp.exp2(s_t - m_new).astype(BF16)
            m_ref[rows, :] = m_new
        start = pl.multiple_of(j * tk, tk)
        pv = jnp.dot(p_ref[...], v_ref[pl.ds(start, tk), :], preferred_element_type=F32)
        alpha = alpha_ref[...]
        for t in range(acc_ref.shape[1] // LANES):
            cols = slice(t * LANES, (t + 1) * LANES)
            acc_ref[:, cols] = acc_ref[:, cols] * alpha + pv[:, cols]

    m_ref[...] = jnp.full(m_ref.shape, -jnp.inf, F32)
    acc_ref[...] = jnp.zeros(acc_ref.shape, F32)
    scores(0, s0_ref)

    def pair(i, carry):
        consume(2 * i, s0_ref, s1_ref)
        consume(2 * i + 1, s1_ref, s0_ref)
        return carry

    lax.fori_loop(0, nk // 2 - 1, pair, 0)
    consume(nk - 2, s0_ref, s1_ref)
    consume(nk - 1, s1_ref, None)
    acc = acc_ref[...]
    o_ref[...] = acc[:, :dv] / acc[:, dv:]


def _mla_attention(q, k, v, *, tq=1024, tk=1024, rb=64):
    b, h, s, dq = q.shape
    dv = MLA_V_HEAD
    assert (s // tk) % 2 == 0
    kern = functools.partial(_attn_kernel, tk=tk, rb=rb)
    return pl.pallas_call(
        kern,
        grid=(b, h, s // tq),
        in_specs=[
            pl.BlockSpec((None, None, tq, dq), lambda bi, hi, i: (bi, hi, i, 0)),
            pl.BlockSpec((None, None, s, dq), lambda bi, hi, i: (bi, hi, 0, 0)),
            pl.BlockSpec((None, None, s, 2 * dv), lambda bi, hi, i: (bi, hi, 0, 0)),
        ],
        out_specs=pl.BlockSpec((None, tq, dv), lambda bi, hi, i: (bi, i, hi)),
        out_shape=jax.ShapeDtypeStruct((b, s, h * dv), F32),
        scratch_shapes=[
            pltpu.VMEM((tq, tk), F32), pltpu.VMEM((tq, tk), F32), pltpu.VMEM((tq, tk), BF16),
            pltpu.VMEM((tq, 2 * dv), F32), pltpu.VMEM((tq, LANES), F32), pltpu.VMEM((tq, LANES), F32),
        ],
        compiler_params=_cparams("parallel", "parallel", "parallel"),
        name="mla_attn",
    )(q, k, v)


def _split3(x):
    x0 = x.astype(BF16)
    r1 = x - x0.astype(F32)
    x1 = r1.astype(BF16)
    x2 = (r1 - x1.astype(F32)).astype(BF16)
    return x0, x1, x2


def _chunk_cumsum(tri, g):
    g0, g1, g2 = _split3(g)
    dot = functools.partial(jnp.dot, preferred_element_type=F32)
    return dot(tri, g0) + (dot(tri, g1) + dot(tri, g2))


def _gla_intra(q, k, v, b2f, b2b, e3, hmask, vmask):
    c = q.shape[0]
    jj = lax.broadcasted_iota(jnp.int32, (GLA_SUB, GLA_SUB, LANES), 0)
    ii = lax.broadcasted_iota(jnp.int32, (GLA_SUB, GLA_SUB, LANES), 1)
    lower = jj <= ii
    row = lax.broadcasted_iota(jnp.int32, (c, LANES), 0)
    v16 = v.astype(BF16)
    outs = []
    for sb in range(c // GLA_SUB):
        r0, r1 = sb * GLA_SUB, (sb + 1) * GLA_SUB
        qs, ks, vs, bf, bb = q[r0:r1], k[r0:r1], v[r0:r1], b2f[r0:r1], b2b[r0:r1]
        d = jnp.where(lower, bf[None, :, :] - bf[:, None, :], bb[None, :, :] - bb[:, None, :])
        p = (qs[None, :, :] * ks[:, None, :]) * jnp.exp2(d)
        w = jnp.dot(p.reshape(GLA_SUB * GLA_SUB, LANES).astype(BF16), e3, preferred_element_type=F32)
        o_sb = jnp.sum(w.reshape(GLA_SUB, GLA_SUB, w.shape[-1]) * vs[:, None, :], axis=0)
        q_parts, k_parts = [], []
        if r0 > 0:
            ref = b2f[r0 - 1:r0]
            q_parts.append(qs * jnp.exp2(bf - ref))
            k_parts.append(k * jnp.exp2(jnp.where(row < r0, ref - b2f, NEG_BIG)))
        if r1 < c:
            ref = b2b[r1:r1 + 1]
            q_parts.append(qs * jnp.exp2(bb - ref))
            k_parts.append(k * jnp.exp2(jnp.where(row >= r1, ref - b2b, NEG_BIG)))
        qt = jnp.concatenate(q_parts, axis=-1)
        kt = jnp.concatenate(k_parts, axis=-1).astype(BF16)
        hm = jnp.concatenate([hmask] * len(q_parts), axis=-1)
        q_stack = jnp.concatenate([qt * hm[hh:hh + 1] for hh in range(GLA_HEADS)], axis=0).astype(BF16)
        a = lax.dot_general(q_stack, kt, (((1,), (1,)), ((), ())), preferred_element_type=F32)
        o_all = jnp.dot(a.astype(BF16), v16, preferred_element_type=F32)
        for hh in range(GLA_HEADS):
            o_sb = o_sb + o_all[hh * GLA_SUB:(hh + 1) * GLA_SUB] * vmask[hh:hh + 1]
        outs.append(o_sb)
    return jnp.concatenate(outs, axis=0)


def _gla_state_kernel(kf_ref, vf_ref, gf_ref, kb_ref, vb_ref, gb_ref, tril_ref, triu_ref, stmask_ref,
                      sf_ref, sb_ref, stf_ref, stb_ref):
    @pl.when(pl.program_id(1) == 0)
    def _():
        stf_ref[...] = jnp.zeros_like(stf_ref)
        stb_ref[...] = jnp.zeros_like(stb_ref)

    nc = kf_ref.shape[0] // GLA_CHUNK
    stmask = stmask_ref[...]

    def step(c, carry):
        for reverse in (False, True):
            if reverse:
                k_ref, v_ref, g_ref, s_ref, st_ref, tri = kb_ref, vb_ref, gb_ref, sb_ref, stb_ref, triu_ref[...]
                ci = nc - 1 - c
            else:
                k_ref, v_ref, g_ref, s_ref, st_ref, tri = kf_ref, vf_ref, gf_ref, sf_ref, stf_ref, tril_ref[...]
                ci = c
            rows = pl.ds(pl.multiple_of(ci * GLA_CHUNK, GLA_CHUNK), GLA_CHUNK)
            beta = _chunk_cumsum(tri, g_ref[rows, :])
            tot = beta[0:1] if reverse else beta[GLA_CHUNK - 1:GLA_CHUNK]
            kd = k_ref[rows, :] * jnp.exp(tot - beta)
            ut = lax.dot_general(v_ref[rows, :].astype(BF16), kd.astype(BF16), (((0,), (0,)), ((), ())),
                                 preferred_element_type=F32)
            st = st_ref[...]
            s_ref[ci] = st.astype(BF16)
            st_ref[...] = st * jnp.exp(tot) + ut * stmask
        return carry

    lax.fori_loop(0, nc, step, 0, unroll=True)


def _gla_out_kernel(q_ref, k_ref, v_ref, gf_ref, gb_ref, r_ref, sf_ref, sb_ref,
                    tril_ref, triu_ref, e3_ref, hmask_ref, vmask_ref, seg_ref, gn_ref, o_ref):
    nc = q_ref.shape[0] // GLA_CHUNK
    tril, triu = tril_ref[...], triu_ref[...]
    e3, hmask, vmask = e3_ref[...], hmask_ref[...], vmask_ref[...]
    seg, gn = seg_ref[...], gn_ref[...]
    log2e = math.log2(math.e)

    def step(c, carry):
        rows = pl.ds(pl.multiple_of(c * GLA_CHUNK, GLA_CHUNK), GLA_CHUNK)
        q, k, v = q_ref[rows, :], k_ref[rows, :], v_ref[rows, :]
        b2f = _chunk_cumsum(tril, gf_ref[rows, :] * log2e)
        b2b = _chunk_cumsum(triu, gb_ref[rows, :] * log2e)
        qd = jnp.concatenate([q * jnp.exp2(b2f), q * jnp.exp2(b2b)], axis=-1).astype(BF16)
        st = jnp.concatenate([sf_ref[c], sb_ref[c]], axis=-1)
        o = lax.dot_general(qd, st, (((1,), (1,)), ((), ())), preferred_element_type=F32)
        o = o + _gla_intra(q, k, v, b2f, b2b, e3, hmask, vmask)
        s0, s1, s2 = _split3(o * o)
        dot = functools.partial(jnp.dot, preferred_element_type=F32)
        ms = dot(s0, seg) + (dot(s1, seg) + dot(s2, seg))
        r = r_ref[rows, :]
        o_ref[rows, :] = (o * lax.rsqrt(ms + NORM_EPS) * gn * (r * jax.nn.sigmoid(r))).astype(o_ref.dtype)
        return carry

    lax.fori_loop(0, nc, step, 0, unroll=4)


def _gla(gq, gk, gv, gf, gb, gr, head_norm, *, tb=512):
    b, s, kw = gq.shape
    vw = gv.shape[-1]
    nb = s // tb
    ncb = tb // GLA_CHUNK
    dk = kw // GLA_HEADS
    dv = vw // GLA_HEADS
    idx = np.arange(GLA_CHUNK)
    tril = jnp.asarray((idx[None, :] <= idx[:, None]).astype(np.float32)).astype(BF16)
    triu = jnp.asarray((idx[None, :] >= idx[:, None]).astype(np.float32)).astype(BF16)
    d_head = np.arange(kw) // dk
    e_head = np.arange(vw) // dv
    e3 = jnp.asarray((d_head[:, None] == e_head[None, :]).astype(np.float32)).astype(BF16)
    hmask = jnp.asarray((np.arange(GLA_HEADS)[:, None] == d_head[None, :]).astype(np.float32))
    vmask = jnp.asarray((np.arange(GLA_HEADS)[:, None] == e_head[None, :]).astype(np.float32))
    stmask = jnp.asarray((e_head[:, None] == d_head[None, :]).astype(np.float32))
    seg = jnp.asarray((e_head[:, None] == e_head[None, :]).astype(np.float32) / dv).astype(BF16)
    tri_spec = _const_spec((GLA_CHUNK, GLA_CHUNK))

    fwd = lambda w: pl.BlockSpec((None, tb, w), lambda bi, i: (bi, i, 0))
    bwd = lambda w: pl.BlockSpec((None, tb, w), lambda bi, i: (bi, nb - 1 - i, 0))
    st_shape = jax.ShapeDtypeStruct((b, s // GLA_CHUNK, vw, kw), BF16)
    s_f, s_b = pl.pallas_call(
        _gla_state_kernel,
        grid=(b, nb),
        in_specs=[fwd(kw), fwd(vw), fwd(kw), bwd(kw), bwd(vw), bwd(kw), tri_spec, tri_spec,
                  _const_spec((vw, kw))],
        out_specs=[pl.BlockSpec((None, ncb, vw, kw), lambda bi, i: (bi, i, 0, 0)),
                   pl.BlockSpec((None, ncb, vw, kw), lambda bi, i: (bi, nb - 1 - i, 0, 0))],
        out_shape=[st_shape, st_shape],
        scratch_shapes=[pltpu.VMEM((vw, kw), F32), pltpu.VMEM((vw, kw), F32)],
        compiler_params=_cparams("parallel", "arbitrary"),
        name="gla_state",
    )(gk, gv, gf, gk, gv, gb, tril, triu, stmask)

    st_spec = pl.BlockSpec((None, ncb, vw, kw), lambda bi, i: (bi, i, 0, 0))
    return pl.pallas_call(
        _gla_out_kernel,
        grid=(b, nb),
        in_specs=[fwd(kw), fwd(kw), fwd(vw), fwd(kw), fwd(kw), fwd(vw), st_spec, st_spec,
                  tri_spec, tri_spec, _const_spec((kw, vw)), _const_spec((GLA_HEADS, kw)),
                  _const_spec((GLA_HEADS, vw)), _const_spec((vw, vw)), _const_spec((1, vw))],
        out_specs=fwd(vw),
        out_shape=jax.ShapeDtypeStruct((b, s, vw), BF16),
        compiler_params=_cparams("parallel", "parallel"),
        name="gla_out",
    )(gq, gk, gv, gf, gb, gr, s_f, s_b, tril, triu, e3, hmask, vmask, seg,
      jnp.tile(head_norm, GLA_HEADS).reshape(1, vw))


def _short_conv_kernel(x_ref, w_ref, b_ref, o_ref):
    x = x_ref[...]
    n = x.shape[0]
    row = lax.broadcasted_iota(jnp.int32, x.shape, 0)
    prev = jnp.where(row == 0, 0.0, pltpu.roll(x, 1, axis=0))
    nxt = jnp.where(row == n - 1, 0.0, pltpu.roll(x, n - 1, axis=0))
    w = w_ref[...]
    o_ref[...] = b_ref[...] + prev * w[0:1] + x * w[1:2] + nxt * w[2:3]


def _short_conv(hy_in, conv_w, conv_b):
    b, s, c3 = hy_in.shape
    c = c3 // 3
    nt = c // LANES
    return pl.pallas_call(
        _short_conv_kernel,
        grid=(b, c3 // LANES),
        in_specs=[
            pl.BlockSpec((None, s, LANES), lambda bi, j: (bi, 0, j)),
            pl.BlockSpec((HY_SHORT, LANES), lambda bi, j: (0, j)),
            pl.BlockSpec((1, LANES), lambda bi, j: (0, j)),
        ],
        out_specs=pl.BlockSpec((None, None, s, LANES), lambda bi, j: (j // nt, bi, 0, j % nt)),
        out_shape=jax.ShapeDtypeStruct((3, b, s, c), F32),
        compiler_params=_cparams("parallel", "parallel"),
        name="hy_short_conv",
    )(hy_in, conv_w, conv_b.reshape(1, c3))


def _filter_kernel(band_ref, w1_ref, b1_ref, f1_ref, w2_ref, b2_ref, f2_ref, w3_ref, delta_ref, o_ref,
                   *, seq):
    tr = o_ref.shape[1]
    c = delta_ref.shape[-1]
    n = pl.program_id(0) * tr + lax.broadcasted_iota(jnp.int32, (tr, LANES), 0)
    lane = lax.broadcasted_iota(jnp.int32, (tr, LANES), 1)
    pos = jnp.where(n < seq, n, 2 * seq - n).astype(F32)
    t = pos / (seq - 1)
    ang = (2.0 * math.pi * pos / seq) * band_ref[...]
    z = jnp.where(lane == 0, t,
                  jnp.where(lane <= HY_BANDS, jnp.cos(ang),
                            jnp.where(lane <= 2 * HY_BANDS, -jnp.sin(ang), 0.0)))
    h = jnp.sin(f1_ref[...] * (jnp.dot(z, w1_ref[...], precision=HIGHEST, preferred_element_type=F32)
                               + b1_ref[...]))
    h = jnp.sin(f2_ref[...] * (jnp.dot(h, w2_ref[...], precision=HIGHEST, preferred_element_type=F32)
                               + b2_ref[...]))
    h = jnp.dot(h, w3_ref[...], precision=HIGHEST, preferred_element_type=F32)
    t_c = jnp.concatenate([t] * (c // LANES), axis=-1)
    window = jnp.exp(-t_c * delta_ref[...]) + HY_DECAY_SHIFT
    live = jnp.concatenate([n] * (c // LANES), axis=-1) != seq
    for o in range(HY_ORDER):
        o_ref[o] = jnp.where(live, h[:, o * c:(o + 1) * c] * window, 0.0)


def _hyena_filters(seq, w1, b1, f1, w2, b2, f2, w3, *, tr=512):
    emb, hid = w1.shape
    c = w3.shape[1] // (HY_ORDER * 2)
    n = 2 * seq
    bands = np.zeros((1, LANES), np.float32)
    bvals = np.linspace(1e-4, HY_BANDS - 1, HY_BANDS, dtype=np.float32)
    bands[0, 1:1 + HY_BANDS] = bvals
    bands[0, 1 + HY_BANDS:1 + 2 * HY_BANDS] = bvals
    max_decay = math.log(HY_DECAY_TARGET) / HY_FAST_DECAY_PCT
    min_decay = math.log(HY_DECAY_TARGET) / HY_SLOW_DECAY_PCT
    deltas = np.abs(np.linspace(min_decay, max_decay, c, dtype=np.float32)).reshape(1, c)
    w1p = jnp.pad(w1, ((0, LANES - emb), (0, 0)))
    w3d = w3.reshape(hid, HY_ORDER, 2, c).transpose(2, 0, 1, 3).reshape(2, hid, HY_ORDER * c)
    half_tiles = seq // tr
    kern = functools.partial(_filter_kernel, seq=seq)
    return pl.pallas_call(
        kern,
        grid=(n // tr,),
        in_specs=[
            _const_spec((1, LANES)),
            _const_spec((LANES, hid)), _const_spec((1, hid)), _const_spec((1, hid)),
            _const_spec((hid, hid)), _const_spec((1, hid)), _const_spec((1, hid)),
            pl.BlockSpec((None, hid, HY_ORDER * c), lambda i: (i // half_tiles, 0, 0)),
            _const_spec((1, c)),
        ],
        out_specs=pl.BlockSpec((HY_ORDER, tr, c), lambda i: (0, i, 0)),
        out_shape=jax.ShapeDtypeStruct((HY_ORDER, n, c), F32),
        compiler_params=_cparams("parallel"),
        name="hy_filter",
    )(jnp.asarray(bands), w1p, b1.reshape(1, hid), f1.reshape(1, hid), w2, b2.reshape(1, hid),
      f2.reshape(1, hid), w3d, jnp.asarray(deltas))


def _dft_tables(p):
    idx = np.arange(p)
    ang = 2.0 * np.pi * np.outer(idx, idx) / p
    fr, fi = np.cos(ang), -np.sin(ang)
    tang = 2.0 * np.pi * np.outer(idx, idx) / (p * p)
    twr, twi = np.cos(tang), -np.sin(tang)
    return fr, fi, twr, twi


def _split_bf16(x):
    hi = x.astype(BF16)
    return hi, (x - hi.astype(F32)).astype(BF16)


def _dft_dot(mat, data, *, split):
    dot = functools.partial(jnp.dot, preferred_element_type=F32)
    if not split:
        return dot(mat.astype(BF16), data.astype(BF16))
    mh, ml = _split_bf16(mat)
    dh, dl = _split_bf16(data)
    return dot(mh, dh) + (dot(mh, dl) + dot(ml, dh))


def _stage_a_kernel(x_ref, m_ref, o_ref, *, split):
    x = jnp.concatenate([x_ref[0], x_ref[1]], axis=0)
    o_ref[0] = _dft_dot(m_ref[...], x, split=split).astype(o_ref.dtype)


def _stage_a(x4, idx, mat, *, split, ct=2048):
    _, g2, hp, cols = x4.shape
    rows = mat.shape[0]
    return pl.pallas_call(
        functools.partial(_stage_a_kernel, split=split),
        grid=(g2 // 2, cols // ct),
        in_specs=[
            pl.BlockSpec((None, 2, hp, ct), lambda g, j: (idx, g, 0, j)),
            _const_spec(mat.shape),
        ],
        out_specs=pl.BlockSpec((1, rows, ct), lambda g, j: (g, 0, j)),
        out_shape=jax.ShapeDtypeStruct((g2 // 2, rows, cols), F32 if split else BF16),
        compiler_params=_cparams("parallel", "parallel"),
        name="hy_dft_a",
    )(x4, mat)


def _fwd_matrix(fr, fi, twr, twi):
    gr = fr * twr - fi * twi
    gi = fr * twi + fi * twr
    return jnp.concatenate([jnp.concatenate([gr, -gi], axis=1), jnp.concatenate([gi, gr], axis=1)], axis=0)


def _stack_pairs(a_ref, j):
    g = a_ref.shape[0]
    re = jnp.concatenate([a_ref[i, 0, j] for i in range(g)], axis=-1)
    im = jnp.concatenate([a_ref[i, 1, j] for i in range(g)], axis=-1)
    return jnp.concatenate([re, im], axis=0)


def _stage_c_kernel(a_ref, fr_ref, fi_ref, twr_ref, twi_ref, o_ref):
    g, _, kk, p, c = a_ref.shape
    fr, fi = fr_ref[...], fi_ref[...]
    for j in range(kk):
        gs = _fwd_matrix(fr, fi, twr_ref[j], twi_ref[j])
        x = _dft_dot(gs, _stack_pairs(a_ref, j), split=True)
        for i in range(g):
            o_ref[i, 0, j] = x[:p, i * c:(i + 1) * c]
            o_ref[i, 1, j] = x[p:, i * c:(i + 1) * c]


def _conv_core_kernel(a_ref, kh_ref, fr_ref, fi_ref, twr_ref, twi_ref, twbr_ref, twbi_ref, o_ref):
    g, _, kk, p, c = a_ref.shape
    fr, fi = fr_ref[...], fi_ref[...]
    for j in range(kk):
        gs = _fwd_matrix(fr, fi, twr_ref[j], twi_ref[j])
        x = _dft_dot(gs, _stack_pairs(a_ref, j), split=False)
        xr, xi = x[:p], x[p:]
        kr = jnp.concatenate([kh_ref[0, j]] * g, axis=-1)
        ki = jnp.concatenate([kh_ref[1, j]] * g, axis=-1)
        y = jnp.concatenate([xr * kr - xi * ki, xr * ki + xi * kr], axis=0)
        br = fr * twbr_ref[j] - fi * twbi_ref[j]
        bi = -(fr * twbi_ref[j] + fi * twbr_ref[j])
        bs = jnp.concatenate([jnp.concatenate([br, -bi], axis=1), jnp.concatenate([bi, br], axis=1)], axis=0)
        r = _dft_dot(bs, y, split=False).astype(o_ref.dtype)
        for i in range(g):
            o_ref[i, 0, j] = r[:p, i * c:(i + 1) * c]
            o_ref[i, 1, j] = r[p:, i * c:(i + 1) * c]


def _tw_specs(p, kk):
    row = pl.BlockSpec((kk, 1, p), lambda i: (i, 0, 0))
    return row


def _stage_c(a5, tabs, *, kk=4):
    g, _, p, _, c = a5.shape
    fr, fi, twr, twi = tabs
    blk = pl.BlockSpec((g, 2, kk, p, c), lambda i: (0, 0, i, 0, 0))
    return pl.pallas_call(
        _stage_c_kernel,
        grid=(p // kk,),
        in_specs=[blk, _const_spec((p, p)), _const_spec((p, p)), _tw_specs(p, kk), _tw_specs(p, kk)],
        out_specs=blk,
        out_shape=jax.ShapeDtypeStruct(a5.shape, F32),
        compiler_params=_cparams("parallel"),
        name="hy_dft_c",
    )(a5, fr, fi, twr.reshape(p, 1, p), twi.reshape(p, 1, p))


def _conv_core(a5, khat, tabs, twb, *, kk=8):
    g, _, p, _, c = a5.shape
    fr, fi, twr, twi = tabs
    twbr, twbi = twb
    blk = pl.BlockSpec((g, 2, kk, p, c), lambda i: (0, 0, i, 0, 0))
    return pl.pallas_call(
        _conv_core_kernel,
        grid=(p // kk,),
        in_specs=[
            blk,
            pl.BlockSpec((2, kk, p, c), lambda i: (0, i, 0, 0)),
            _const_spec((p, p)), _const_spec((p, p)), _tw_specs(p, kk), _tw_specs(p, kk),
            pl.BlockSpec((kk, p, p), lambda i: (i, 0, 0)),
            pl.BlockSpec((kk, p, p), lambda i: (i, 0, 0)),
        ],
        out_specs=blk,
        out_shape=jax.ShapeDtypeStruct(a5.shape, BF16),
        compiler_params=_cparams("parallel"),
        name="hy_conv_core",
    )(a5, khat, fr, fi, twr.reshape(p, 1, p), twi.reshape(p, 1, p), twbr, twbi)


def _stage_b_kernel(r_ref, m_ref, u_ref, gate_ref, skip_ref, o_ref):
    y = _dft_dot(m_ref[...], r_ref[0], split=False)
    hp = u_ref.shape[1]
    skip = skip_ref[...]
    o_ref[0] = gate_ref[0] * (y[:hp] + skip * u_ref[0])
    o_ref[1] = gate_ref[1] * (y[hp:] + skip * u_ref[1])


def _stage_b(r3, mat, u4, u_idx, gate4, gate_idx, skip_row, *, ct=2048):
    g, rows, cols = r3.shape
    hp = mat.shape[0] // 2
    nb = u4.shape[1]
    return pl.pallas_call(
        _stage_b_kernel,
        grid=(g, cols // ct),
        in_specs=[
            pl.BlockSpec((1, rows, ct), lambda gi, j: (gi, 0, j)),
            _const_spec(mat.shape),
            pl.BlockSpec((None, 2, hp, ct), lambda gi, j: (u_idx, gi, 0, j)),
            pl.BlockSpec((None, 2, hp, ct), lambda gi, j: (gate_idx, gi, 0, j)),
            pl.BlockSpec((1, ct), lambda gi, j: (0, j)),
        ],
        out_specs=pl.BlockSpec((2, hp, ct), lambda gi, j: (gi, 0, j)),
        out_shape=jax.ShapeDtypeStruct((nb, hp, cols), F32),
        compiler_params=_cparams("parallel", "parallel"),
        name="hy_dft_b",
    )(r3, mat, u4, gate4, skip_row)


def _hyena(hy_in, conv_w, conv_b, w1, b1, f1, w2, b2, f2, w3, skip):
    b, s, c3 = hy_in.shape
    c = c3 // 3
    p = math.isqrt(2 * s)
    assert p * p == 2 * s and b % 2 == 0
    hp = p // 2
    cols = p * c
    fr, fi, twr, twi = _dft_tables(p)
    scale = 1.0 / (p * p)
    mat_a = np.block([[fr[:, :hp], -fi[:, :hp]], [fi[:, :hp], fr[:, :hp]]]).astype(np.float32)
    mat_k = np.concatenate([fr, fi], axis=0).astype(np.float32)
    br, bi = fr[:hp] * scale, -fi[:hp] * scale
    mat_b = np.block([[br, -bi], [bi, br]]).astype(np.float32)
    tabs = tuple(jnp.asarray(t.astype(np.float32)) for t in (fr, fi, twr, twi))
    twb = tuple(jnp.broadcast_to(t[:, :, None], (p, p, p)) for t in tabs[2:])

    uvx = _short_conv(hy_in, conv_w, conv_b).reshape(3, b, hp, cols)
    kern = _hyena_filters(s, w1, b1, f1, w2, b2, f2, w3)
    ka = _stage_a(kern.reshape(1, HY_ORDER * 2, hp, cols), 0, jnp.asarray(mat_k), split=True)
    khat = _stage_c(ka.reshape(HY_ORDER, 2, p, p, c), tabs)

    skip_rows = jnp.tile(skip, (1, p))
    g = b // 2

    def long_conv(u4, u_idx, gate_idx, order):
        a = _stage_a(u4, u_idx, jnp.asarray(mat_a), split=False)
        r = _conv_core(a.reshape(g, 2, p, p, c), khat[order], tabs, twb)
        return _stage_b(r.reshape(g, 2 * p, cols), jnp.asarray(mat_b), u4, u_idx, uvx, gate_idx,
                        skip_rows[order:order + 1])

    z = long_conv(uvx, 0, 1, 0)
    y = long_conv(z[None], 0, 2, 1)
    return y.reshape(b, s, c)


def _out_proj_kernel(x_ref, hy_ref, mla_ref, gla_ref, ghy_ref, gmla_ref, w_ref, o_ref):
    y_hy = _rms(hy_ref[...], ghy_ref[...]).astype(BF16)
    y_mla = _rms(mla_ref[...], gmla_ref[...]).astype(BF16)
    cat = jnp.concatenate([y_hy, y_mla, gla_ref[...]], axis=-1)
    o_ref[...] = x_ref[...] + jnp.dot(cat, w_ref[...], preferred_element_type=F32)


def _out_proj(x2d, y_hy, o_mla, y_gla, g_hy, g_mla, w_out, *, tm=512):
    t, d = x2d.shape
    c_hy, c_mla, c_gla = y_hy.shape[1], o_mla.shape[1], y_gla.shape[1]
    tok = lambda w: pl.BlockSpec((tm, w), lambda i: (i, 0))
    return pl.pallas_call(
        _out_proj_kernel,
        grid=(t // tm,),
        in_specs=[
            tok(d), tok(c_hy), tok(c_mla), tok(c_gla),
            _const_spec((1, c_hy)), _const_spec((1, c_mla)), _const_spec(w_out.shape),
        ],
        out_specs=tok(d),
        out_shape=jax.ShapeDtypeStruct((t, d), F32),
        compiler_params=_cparams("parallel"),
        name="out_proj",
    )(x2d, y_hy, o_mla, y_gla, g_hy.reshape(1, -1), g_mla.reshape(1, -1), w_out.astype(BF16))


def kernel(x, ffn1_norm, ffn1_w_gate, ffn1_w_up, ffn1_w_down, mix_norm, w_in, hy_conv_w, hy_conv_b, hy_filt_w1, hy_filt_b1, hy_filt_freq1, hy_filt_w2, hy_filt_b2, hy_filt_freq2, hy_filt_w3, hy_skip, hy_out_norm, mla_q_norm, mla_w_uq, mla_kv_norm, mla_w_ukv, mla_out_norm, gla_w_gate_fwd, gla_b_gate_fwd, gla_w_gate_bwd, gla_b_gate_bwd, gla_head_norm, w_out, ffn2_norm, ffn2_w_gate, ffn2_w_up, ffn2_w_down, final_norm):
    b, s, d = x.shape
    depth = ffn1_norm.shape[0]
    t = b * s
    for l in range(depth):
        x2 = _ffn(x.reshape(t, d), ffn1_norm[l], ffn1_w_gate[l], ffn1_w_up[l], ffn1_w_down[l], final_norm,
                  final=False)
        (hy_in, q, k, v, gq, gk, gv, gf, gb, gr) = _in_proj(
            x2.reshape(b, s, d), mix_norm[l], w_in[l], mla_q_norm[l], mla_w_uq[l], mla_kv_norm[l],
            mla_w_ukv[l], gla_w_gate_fwd[l], gla_b_gate_fwd[l], gla_w_gate_bwd[l], gla_b_gate_bwd[l])
        y_hy = _hyena(hy_in, hy_conv_w[l], hy_conv_b[l], hy_filt_w1[l], hy_filt_b1[l], hy_filt_freq1[l],
                      hy_filt_w2[l], hy_filt_b2[l], hy_filt_freq2[l], hy_filt_w3[l], hy_skip[l])
        o_mla = _mla_attention(q, k, v)
        y_gla = _gla(gq, gk, gv, gf, gb, gr, gla_head_norm[l])
        x3 = _out_proj(x2, y_hy.reshape(t, -1), o_mla.reshape(t, -1), y_gla.reshape(t, -1),
                       hy_out_norm[l], mla_out_norm[l], w_out[l])
        x = _ffn(x3, ffn2_norm[l], ffn2_w_gate[l], ffn2_w_up[l], ffn2_w_down[l], final_norm,
                 final=(l == depth - 1)).reshape(b, s, d)
    return x
```

```python
import functools
import math

import jax
import jax.numpy as jnp
import numpy as np
from jax import lax
from jax.experimental import pallas as pl
from jax.experimental.pallas import tpu as pltpu

F32 = jnp.float32
BF16 = jnp.bfloat16
HIGHEST = lax.Precision.HIGHEST

NORM_EPS = 1e-6
HY_ORDER = 2
HY_SHORT = 3
HY_BANDS = 16
HY_FAST_DECAY_PCT = 0.3
HY_SLOW_DECAY_PCT = 1.5
HY_DECAY_TARGET = 1e-2
HY_DECAY_SHIFT = 0.0
MLA_HEADS = 4
MLA_QK_NOPE = 128
MLA_QK_ROPE = 64
MLA_V_HEAD = 128
ROPE_BASE = 10000.0
GLA_HEADS = 4
GLA_GATE_RANK = 16
GLA_GATE_NORM = 16.0

LANES = 128
GLA_CHUNK = 64
GLA_SUB = 16
VMEM_LIMIT = 56 * 1024 * 1024

NEG_BIG = -1e30


def _cparams(*sem):
    return pltpu.CompilerParams(dimension_semantics=sem, vmem_limit_bytes=VMEM_LIMIT)


def _rms(x, g):
    ms = jnp.mean(x * x, axis=-1, keepdims=True)
    return x * lax.rsqrt(ms + NORM_EPS) * g


def _const_spec(shape):
    nd = len(shape)
    return pl.BlockSpec(shape, lambda *_: (0,) * nd)


def _ffn_kernel(x_ref, g_ref, wg_ref, wu_ref, wd_ref, fg_ref, o_ref, *, ff_chunk, final):
    _ffn_body(x_ref[...], g_ref, wg_ref, wu_ref, wd_ref, fg_ref, o_ref, ff_chunk=ff_chunk, final=final)


def _mix_ffn_kernel(x_ref, hy_ref, mla_ref, gla_ref, ghy_ref, gmla_ref, wout_ref,
                    g_ref, wg_ref, wu_ref, wd_ref, fg_ref, o_ref, *, ff_chunk, final):
    y_hy = _rms(hy_ref[...], ghy_ref[...]).astype(BF16)
    y_mla = _rms(mla_ref[...], gmla_ref[...]).astype(BF16)
    cat = jnp.concatenate([y_hy, y_mla, gla_ref[...]], axis=-1)
    x = x_ref[...] + jnp.dot(cat, wout_ref[...], preferred_element_type=F32)
    _ffn_body(x, g_ref, wg_ref, wu_ref, wd_ref, fg_ref, o_ref, ff_chunk=ff_chunk, final=final)


def _ffn_body(x, g_ref, wg_ref, wu_ref, wd_ref, fg_ref, o_ref, *, ff_chunk, final):
    xn = _rms(x, g_ref[...]).astype(BF16)
    d_ff = wg_ref.shape[1]
    acc = jnp.zeros_like(x)
    for c in range(0, d_ff, ff_chunk):
        gate = jnp.dot(xn, wg_ref[:, c:c + ff_chunk], preferred_element_type=F32)
        up = jnp.dot(xn, wu_ref[:, c:c + ff_chunk], preferred_element_type=F32)
        h = (gate * jax.nn.sigmoid(gate) * up).astype(BF16)
        acc = acc + jnp.dot(h, wd_ref[c:c + ff_chunk, :], preferred_element_type=F32)
    y = x + 0.5 * acc
    if final:
        y = _rms(y, fg_ref[...])
    o_ref[...] = y


def _resident_spec(shape):
    nd = len(shape)
    return pl.BlockSpec(shape, lambda *_: (0,) * nd, pipeline_mode=pl.Buffered(1))


def _ffn(x2d, norm_g, w_gate, w_up, w_down, final_g, *, final, mix=None, tm=512, ff_chunk=256):
    t, d = x2d.shape
    d_ff = w_gate.shape[1]
    tok = lambda w: pl.BlockSpec((tm, w), lambda i: (i, 0))
    ffn_specs = [_const_spec((1, d)), _resident_spec((d, d_ff)), _resident_spec((d, d_ff)),
                 _resident_spec((d_ff, d)), _const_spec((1, d))]
    ffn_args = (norm_g.reshape(1, d), w_gate.astype(BF16), w_up.astype(BF16), w_down.astype(BF16),
                final_g.reshape(1, d))
    if mix is None:
        kern, name = _ffn_kernel, "ffn"
        in_specs = [tok(d)] + ffn_specs
        args = (x2d,) + ffn_args
    else:
        y_hy, o_mla, y_gla, g_hy, g_mla, w_out = mix
        kern, name = _mix_ffn_kernel, "mix_ffn"
        in_specs = [tok(d), tok(y_hy.shape[1]), tok(o_mla.shape[1]), tok(y_gla.shape[1]),
                    _const_spec((1, y_hy.shape[1])), _const_spec((1, o_mla.shape[1])),
                    _resident_spec(w_out.shape)] + ffn_specs
        args = (x2d, y_hy, o_mla, y_gla, g_hy.reshape(1, -1), g_mla.reshape(1, -1),
                w_out.astype(BF16)) + ffn_args
    return pl.pallas_call(
        functools.partial(kern, ff_chunk=ff_chunk, final=final),
        grid=(t // tm,),
        in_specs=in_specs,
        out_specs=tok(d),
        out_shape=jax.ShapeDtypeStruct((t, d), F32),
        compiler_params=_cparams("parallel"),
        name=name,
    )(*args)


def _log_sigmoid(z):
    return jnp.minimum(z, 0.0) - jnp.log(1.0 + jnp.exp(-jnp.abs(z)))


def _in_proj_kernel(x_ref, g_ref, win_ref, qn_ref, wuq_ref, kvn_ref, wukv_ref, cos_ref, sin_ref,
                    wgate_ref, bgate_ref,
                    hy_ref, q_ref, k_ref, v_ref, gq_ref, gk_ref, gv_ref, gf_ref, gb_ref, gr_ref,
                    *, off, q_scale, gla_q_scale):
    x = x_ref[0]
    xn = _rms(x, g_ref[...]).astype(BF16)
    proj = jnp.dot(xn, win_ref[...], preferred_element_type=F32)

    def blk(name):
        lo, hi = off[name]
        return proj[:, lo:hi]

    hy_ref[0] = blk("hy")

    qn = _rms(blk("cq"), qn_ref[...]).astype(BF16)
    qf = jnp.dot(qn, wuq_ref[...], preferred_element_type=F32)
    kvn = _rms(blk("ckv"), kvn_ref[...]).astype(BF16)
    kvf = jnp.dot(kvn, wukv_ref[...], preferred_element_type=F32)
    cos = cos_ref[...]
    sin = sin_ref[...]
    k_rope = blk("kr") * cos + blk("kr_rot") * sin
    ones = jnp.ones_like(k_rope)
    qw = 3 * LANES
    kvw = 2 * LANES
    for h in range(MLA_HEADS):
        q_nope = qf[:, h * qw:h * qw + LANES]
        q_rope = (qf[:, h * qw + LANES:h * qw + 2 * LANES] * cos
                  + qf[:, h * qw + 2 * LANES:(h + 1) * qw] * sin)
        q_ref[0, h] = (jnp.concatenate([q_nope, q_rope], axis=-1) * q_scale).astype(BF16)
        k_ref[0, h] = jnp.concatenate([kvf[:, h * kvw:h * kvw + LANES], k_rope], axis=-1).astype(BF16)
        v_ref[0, h] = jnp.concatenate([kvf[:, h * kvw + LANES:(h + 1) * kvw], ones], axis=-1).astype(BF16)

    gq_ref[0] = blk("gq") * gla_q_scale
    gk_ref[0] = blk("gk")
    gv_ref[0] = blk("gv")
    z = jnp.dot(blk("gate"), wgate_ref[...], precision=HIGHEST, preferred_element_type=F32) + bgate_ref[...]
    gates = _log_sigmoid(z) / GLA_GATE_NORM
    gk_w = gf_ref.shape[-1]
    gf_ref[0] = gates[:, :gk_w]
    gb_ref[0] = gates[:, gk_w:]
    gr_ref[0] = blk("gr")


def _rope_rot_cols(w):
    half = w.shape[-1] // 2
    return jnp.concatenate([-w[..., half:], w[..., :half]], axis=-1)


def _pad_cols(w, width):
    return jnp.pad(w, ((0, 0), (0, width - w.shape[-1])))


def _in_proj(x, mix_g, w_in, q_norm, w_uq, kv_norm, w_ukv, w_gf, b_gf, w_gb, b_gb, *, tm=512):
    b, s, d = x.shape
    hy_in = 3 * (d // 4)
    q_lora = d // 4
    kv_lora = d // 8
    gla_w = d // 4
    gla_key = gla_w // 2
    rope = MLA_QK_ROPE
    splits = (hy_in, q_lora, kv_lora, rope, gla_key, gla_key, gla_w, GLA_GATE_RANK, GLA_GATE_RANK, gla_w)
    offs = np.concatenate([[0], np.cumsum(splits)])
    (w_hy, w_cq, w_ckv, w_kr, w_gq, w_gk, w_gv, w_gfl, w_gbl, w_gr) = [
        w_in[:, offs[i]:offs[i + 1]] for i in range(len(splits))]
    pieces = [
        ("hy", w_hy), ("cq", w_cq), ("ckv", w_ckv),
        ("kr", _pad_cols(w_kr, LANES)), ("kr_rot", _pad_cols(_rope_rot_cols(w_kr), LANES)),
        ("gq", w_gq), ("gk", w_gk), ("gv", w_gv),
        ("gate", _pad_cols(jnp.concatenate([w_gfl, w_gbl], axis=1), LANES)),
        ("gr", w_gr),
    ]
    off = {}
    pos = 0
    for name, w in pieces:
        off[name] = (pos, pos + w.shape[1])
        pos += w.shape[1]
    win_r = jnp.concatenate([w for _, w in pieces], axis=1).astype(BF16)
    n_in = win_r.shape[1]

    dqk = MLA_QK_NOPE + MLA_QK_ROPE
    wq = w_uq.reshape(q_lora, MLA_HEADS, dqk)
    wq_nope = wq[..., :MLA_QK_NOPE]
    wq_rope = wq[..., MLA_QK_NOPE:]
    zpad = jnp.zeros((q_lora, MLA_HEADS, LANES - rope), F32)
    wuq_r = jnp.concatenate([wq_nope, wq_rope, zpad, _rope_rot_cols(wq_rope), zpad], axis=-1)
    wuq_r = wuq_r.reshape(q_lora, MLA_HEADS * 3 * LANES).astype(BF16)
    wukv_r = w_ukv.astype(BF16)

    half = rope // 2
    inv = ROPE_BASE ** (-jnp.arange(half, dtype=F32) * 2.0 / rope)
    ang = jnp.arange(s, dtype=F32)[:, None] * inv[None, :]
    zt = jnp.zeros((s, LANES - rope), F32)
    cos_t = jnp.concatenate([jnp.cos(ang), jnp.cos(ang), zt], axis=1)
    sin_t = jnp.concatenate([jnp.sin(ang), jnp.sin(ang), zt], axis=1)

    wgate = jnp.zeros((LANES, 2 * gla_key), F32)
    wgate = wgate.at[:GLA_GATE_RANK, :gla_key].set(w_gf)
    wgate = wgate.at[GLA_GATE_RANK:2 * GLA_GATE_RANK, gla_key:].set(w_gb)
    bgate = jnp.concatenate([b_gf, b_gb]).reshape(1, 2 * gla_key)

    kern = functools.partial(_in_proj_kernel, off=off, q_scale=float(dqk) ** -0.5 * math.log2(math.e),
                             gla_q_scale=float(gla_key // GLA_HEADS) ** -0.5)
    tok = lambda w: pl.BlockSpec((1, tm, w), lambda bi, i: (bi, i, 0))
    head = lambda w: pl.BlockSpec((1, MLA_HEADS, tm, w), lambda bi, i: (bi, 0, i, 0))
    outs = pl.pallas_call(
        kern,
        grid=(b, s // tm),
        in_specs=[
            tok(d),
            _const_spec((1, d)),
            _const_spec((d, n_in)),
            _const_spec((1, q_lora)),
            _const_spec((q_lora, MLA_HEADS * 3 * LANES)),
            _const_spec((1, kv_lora)),
            _const_spec((kv_lora, MLA_HEADS * 2 * LANES)),
            pl.BlockSpec((tm, LANES), lambda bi, i: (i, 0)),
            pl.BlockSpec((tm, LANES), lambda bi, i: (i, 0)),
            _const_spec((LANES, 2 * gla_key)),
            _const_spec((1, 2 * gla_key)),
        ],
        out_specs=[
            tok(hy_in),
            head(2 * LANES), head(2 * LANES), head(2 * LANES),
            tok(gla_key), tok(gla_key), tok(gla_w), tok(gla_key), tok(gla_key), tok(gla_w),
        ],
        out_shape=[
            jax.ShapeDtypeStruct((b, s, hy_in), F32),
            jax.ShapeDtypeStruct((b, MLA_HEADS, s, 2 * LANES), BF16),
            jax.ShapeDtypeStruct((b, MLA_HEADS, s, 2 * LANES), BF16),
            jax.ShapeDtypeStruct((b, MLA_HEADS, s, 2 * LANES), BF16),
            jax.ShapeDtypeStruct((b, s, gla_key), F32),
            jax.ShapeDtypeStruct((b, s, gla_key), F32),
            jax.ShapeDtypeStruct((b, s, gla_w), F32),
            jax.ShapeDtypeStruct((b, s, gla_key), F32),
            jax.ShapeDtypeStruct((b, s, gla_key), F32),
            jax.ShapeDtypeStruct((b, s, gla_w), F32),
        ],
        compiler_params=_cparams("parallel", "parallel"),
        name="in_proj",
    )(x, mix_g.reshape(1, d), win_r, q_norm.reshape(1, -1), wuq_r, kv_norm.reshape(1, -1), wukv_r,
      cos_t, sin_t, wgate, bgate)
    return outs


def _attn_kernel(q_ref, k_ref, v_ref, o_ref, s0_ref, s1_ref, p_ref, acc_ref, m_ref, alpha_ref, *, tk, rb):
    q = q_ref[...]
    tq = q.shape[0]
    nk = k_ref.shape[0] // tk
    dv = o_ref.shape[-1]

    def scores(j, dst_ref):
        start = pl.multiple_of(j * tk, tk)
        dst_ref[...] = lax.dot_general(q, k_ref[pl.ds(start, tk), :], (((1,), (1,)), ((), ())),
                                       preferred_element_type=F32)

    def consume(j, cur_ref, nxt_ref):
        if nxt_ref is not None:
            scores(j + 1, nxt_ref)
        for r in range(tq // rb):
            rows = slice(r * rb, (r + 1) * rb)
            tiles = [cur_ref[rows, t * LANES:(t + 1) * LANES] for t in range(tk // LANES)]
            mx = functools.reduce(jnp.maximum, tiles)
            m_old = m_ref[rows, :]
            m_new = jnp.maximum(m_old, jnp.broadcast_to(jnp.max(mx, axis=-1, keepdims=True), m_old.shape))
            alpha_ref[rows, :] = jnp.exp2(m_old - m_new)
            for t, s_t in enumerate(tiles):
                p_ref[rows, t * LANES:(t + 1) * LANES] = jnp.exp2(s_t - m_new).astype(BF16)
            m_ref[rows, :] = m_new
        start = pl.multiple_of(j * tk, tk)
        pv = jnp.dot(p_ref[...], v_ref[pl.ds(start, tk), :], preferred_element_type=F32)
        alpha = alpha_ref[...]
        for t in range(acc_ref.shape[1] // LANES):
            cols = slice(t * LANES, (t + 1) * LANES)
            acc_ref[:, cols] = acc_ref[:, cols] * alpha + pv[:, cols]

    m_ref[...] = jnp.full(m_ref.shape, -jnp.inf, F32)
    acc_ref[...] = jnp.zeros(acc_ref.shape, F32)
    scores(0, s0_ref)

    def pair(i, carry):
        consume(2 * i, s0_ref, s1_ref)
        consume(2 * i + 1, s1_ref, s0_ref)
        return carry

    lax.fori_loop(0, nk // 2 - 1, pair, 0)
    consume(nk - 2, s0_ref, s1_ref)
    consume(nk - 1, s1_ref, None)
    acc = acc_ref[...]
    o_ref[...] = acc[:, :dv] / acc[:, dv:]


def _mla_attention(q, k, v, *, tq=1024, tk=2048, rb=64):
    b, h, s, dq = q.shape
    dv = MLA_V_HEAD
    tk = min(tk, s // 2)
    assert (s // tk) % 2 == 0
    kern = functools.partial(_attn_kernel, tk=tk, rb=rb)
    return pl.pallas_call(
        kern,
        grid=(b, h, s // tq),
        in_specs=[
            pl.BlockSpec((None, None, tq, dq), lambda bi, hi, i: (bi, hi, i, 0)),
            pl.BlockSpec((None, None, s, dq), lambda bi, hi, i: (bi, hi, 0, 0)),
            pl.BlockSpec((None, None, s, 2 * dv), lambda bi, hi, i: (bi, hi, 0, 0)),
        ],
        out_specs=pl.BlockSpec((None, tq, dv), lambda bi, hi, i: (bi, i, hi)),
        out_shape=jax.ShapeDtypeStruct((b, s, h * dv), F32),
        scratch_shapes=[
            pltpu.VMEM((tq, tk), F32), pltpu.VMEM((tq, tk), F32), pltpu.VMEM((tq, tk), BF16),
            pltpu.VMEM((tq, 2 * dv), F32), pltpu.VMEM((tq, LANES), F32), pltpu.VMEM((tq, LANES), F32),
        ],
        compiler_params=_cparams("parallel", "parallel", "parallel"),
        name="mla_attn",
    )(q, k, v)


def _split3(x):
    x0 = x.astype(BF16)
    r1 = x - x0.astype(F32)
    x1 = r1.astype(BF16)
    x2 = (r1 - x1.astype(F32)).astype(BF16)
    return x0, x1, x2


def _chunk_cumsum(tri, g):
    g0, g1, g2 = _split3(g)
    dot = functools.partial(jnp.dot, preferred_element_type=F32)
    return dot(tri, g0) + (dot(tri, g1) + dot(tri, g2))


def _gla_intra(q, k, v, b2f, b2b, e3, hmask, vmask):
    c = q.shape[0]
    jj = lax.broadcasted_iota(jnp.int32, (GLA_SUB, GLA_SUB, LANES), 0)
    ii = lax.broadcasted_iota(jnp.int32, (GLA_SUB, GLA_SUB, LANES), 1)
    lower = jj <= ii
    row = lax.broadcasted_iota(jnp.int32, (c, LANES), 0)
    v16 = v.astype(BF16)
    outs = []
    for sb in range(c // GLA_SUB):
        r0, r1 = sb * GLA_SUB, (sb + 1) * GLA_SUB
        qs, ks, vs, bf, bb = q[r0:r1], k[r0:r1], v[r0:r1], b2f[r0:r1], b2b[r0:r1]
        d = jnp.where(lower, bf[None, :, :] - bf[:, None, :], bb[None, :, :] - bb[:, None, :])
        p = (qs[None, :, :] * ks[:, None, :]) * jnp.exp2(d)
        w = jnp.dot(p.reshape(GLA_SUB * GLA_SUB, LANES).astype(BF16), e3, preferred_element_type=F32)
        o_sb = jnp.sum(w.reshape(GLA_SUB, GLA_SUB, w.shape[-1]) * vs[:, None, :], axis=0)
        q_parts, k_parts = [], []
        if r0 > 0:
            ref = b2f[r0 - 1:r0]
            q_parts.append(qs * jnp.exp2(bf - ref))
            k_parts.append(k * jnp.exp2(jnp.where(row < r0, ref - b2f, NEG_BIG)))
        if r1 < c:
            ref = b2b[r1:r1 + 1]
            q_parts.append(qs * jnp.exp2(bb - ref))
            k_parts.append(k * jnp.exp2(jnp.where(row >= r1, ref - b2b, NEG_BIG)))
        qt = jnp.concatenate(q_parts, axis=-1)
        kt = jnp.concatenate(k_parts, axis=-1).astype(BF16)
        hm = jnp.concatenate([hmask] * len(q_parts), axis=-1)
        q_stack = jnp.concatenate([qt * hm[hh:hh + 1] for hh in range(GLA_HEADS)], axis=0).astype(BF16)
        a = lax.dot_general(q_stack, kt, (((1,), (1,)), ((), ())), preferred_element_type=F32)
        o_all = jnp.dot(a.astype(BF16), v16, preferred_element_type=F32)
        for hh in range(GLA_HEADS):
            o_sb = o_sb + o_all[hh * GLA_SUB:(hh + 1) * GLA_SUB] * vmask[hh:hh + 1]
        outs.append(o_sb)
    return jnp.concatenate(outs, axis=0)


def _gla_state_kernel(kf_ref, vf_ref, gf_ref, kb_ref, vb_ref, gb_ref, tril_ref, triu_ref, stmask_ref,
                      sf_ref, sb_ref, stf_ref, stb_ref):
    @pl.when(pl.program_id(1) == 0)
    def _():
        stf_ref[...] = jnp.zeros_like(stf_ref)
        stb_ref[...] = jnp.zeros_like(stb_ref)

    nc = kf_ref.shape[0] // GLA_CHUNK
    stmask = stmask_ref[...]

    def step(c, carry):
        for reverse in (False, True):
            if reverse:
                k_ref, v_ref, g_ref, s_ref, st_ref, tri = kb_ref, vb_ref, gb_ref, sb_ref, stb_ref, triu_ref[...]
                ci = nc - 1 - c
            else:
                k_ref, v_ref, g_ref, s_ref, st_ref, tri = kf_ref, vf_ref, gf_ref, sf_ref, stf_ref, tril_ref[...]
                ci = c
            rows = pl.ds(pl.multiple_of(ci * GLA_CHUNK, GLA_CHUNK), GLA_CHUNK)
            beta = _chunk_cumsum(tri, g_ref[rows, :])
            tot = beta[0:1] if reverse else beta[GLA_CHUNK - 1:GLA_CHUNK]
            kd = k_ref[rows, :] * jnp.exp(tot - beta)
            ut = lax.dot_general(v_ref[rows, :].astype(BF16), kd.astype(BF16), (((0,), (0,)), ((), ())),
                                 preferred_element_type=F32)
            st = st_ref[...]
            s_ref[ci] = st.astype(BF16)
            st_ref[...] = st * jnp.exp(tot) + ut * stmask
        return carry

    lax.fori_loop(0, nc, step, 0, unroll=True)


def _gla_out_kernel(q_ref, k_ref, v_ref, gf_ref, gb_ref, r_ref, sf_ref, sb_ref,
                    tril_ref, triu_ref, e3_ref, hmask_ref, vmask_ref, seg_ref, gn_ref, o_ref):
    nc = q_ref.shape[0] // GLA_CHUNK
    tril, triu = tril_ref[...], triu_ref[...]
    e3, hmask, vmask = e3_ref[...], hmask_ref[...], vmask_ref[...]
    seg, gn = seg_ref[...], gn_ref[...]
    log2e = math.log2(math.e)

    def step(c, carry):
        rows = pl.ds(pl.multiple_of(c * GLA_CHUNK, GLA_CHUNK), GLA_CHUNK)
        q, k, v = q_ref[rows, :], k_ref[rows, :], v_ref[rows, :]
        b2f = _chunk_cumsum(tril, gf_ref[rows, :] * log2e)
        b2b = _chunk_cumsum(triu, gb_ref[rows, :] * log2e)
        qd = jnp.concatenate([q * jnp.exp2(b2f), q * jnp.exp2(b2b)], axis=-1).astype(BF16)
        st = jnp.concatenate([sf_ref[c], sb_ref[c]], axis=-1)
        o = lax.dot_general(qd, st, (((1,), (1,)), ((), ())), preferred_element_type=F32)
        o = o + _gla_intra(q, k, v, b2f, b2b, e3, hmask, vmask)
        s0, s1, s2 = _split3(o * o)
        dot = functools.partial(jnp.dot, preferred_element_type=F32)
        ms = dot(s0, seg) + (dot(s1, seg) + dot(s2, seg))
        r = r_ref[rows, :]
        o_ref[rows, :] = (o * lax.rsqrt(ms + NORM_EPS) * gn * (r * jax.nn.sigmoid(r))).astype(o_ref.dtype)
        return carry

    lax.fori_loop(0, nc, step, 0, unroll=4)


def _gla(gq, gk, gv, gf, gb, gr, head_norm, *, tb=512):
    b, s, kw = gq.shape
    vw = gv.shape[-1]
    nb = s // tb
    ncb = tb // GLA_CHUNK
    dk = kw // GLA_HEADS
    dv = vw // GLA_HEADS
    idx = np.arange(GLA_CHUNK)
    tril = jnp.asarray((idx[None, :] <= idx[:, None]).astype(np.float32)).astype(BF16)
    triu = jnp.asarray((idx[None, :] >= idx[:, None]).astype(np.float32)).astype(BF16)
    d_head = np.arange(kw) // dk
    e_head = np.arange(vw) // dv
    e3 = jnp.asarray((d_head[:, None] == e_head[None, :]).astype(np.float32)).astype(BF16)
    hmask = jnp.asarray((np.arange(GLA_HEADS)[:, None] == d_head[None, :]).astype(np.float32))
    vmask = jnp.asarray((np.arange(GLA_HEADS)[:, None] == e_head[None, :]).astype(np.float32))
    stmask = jnp.asarray((e_head[:, None] == d_head[None, :]).astype(np.float32))
    seg = jnp.asarray((e_head[:, None] == e_head[None, :]).astype(np.float32) / dv).astype(BF16)
    tri_spec = _const_spec((GLA_CHUNK, GLA_CHUNK))

    fwd = lambda w: pl.BlockSpec((None, tb, w), lambda bi, i: (bi, i, 0))
    bwd = lambda w: pl.BlockSpec((None, tb, w), lambda bi, i: (bi, nb - 1 - i, 0))
    st_shape = jax.ShapeDtypeStruct((b, s // GLA_CHUNK, vw, kw), BF16)
    s_f, s_b = pl.pallas_call(
        _gla_state_kernel,
        grid=(b, nb),
        in_specs=[fwd(kw), fwd(vw), fwd(kw), bwd(kw), bwd(vw), bwd(kw), tri_spec, tri_spec,
                  _const_spec((vw, kw))],
        out_specs=[pl.BlockSpec((None, ncb, vw, kw), lambda bi, i: (bi, i, 0, 0)),
                   pl.BlockSpec((None, ncb, vw, kw), lambda bi, i: (bi, nb - 1 - i, 0, 0))],
        out_shape=[st_shape, st_shape],
        scratch_shapes=[pltpu.VMEM((vw, kw), F32), pltpu.VMEM((vw, kw), F32)],
        compiler_params=_cparams("parallel", "arbitrary"),
        name="gla_state",
    )(gk, gv, gf, gk, gv, gb, tril, triu, stmask)

    st_spec = pl.BlockSpec((None, ncb, vw, kw), lambda bi, i: (bi, i, 0, 0))
    return pl.pallas_call(
        _gla_out_kernel,
        grid=(b, nb),
        in_specs=[fwd(kw), fwd(kw), fwd(vw), fwd(kw), fwd(kw), fwd(vw), st_spec, st_spec,
                  tri_spec, tri_spec, _const_spec((kw, vw)), _const_spec((GLA_HEADS, kw)),
                  _const_spec((GLA_HEADS, vw)), _const_spec((vw, vw)), _const_spec((1, vw))],
        out_specs=fwd(vw),
        out_shape=jax.ShapeDtypeStruct((b, s, vw), BF16),
        compiler_params=_cparams("parallel", "parallel"),
        name="gla_out",
    )(gq, gk, gv, gf, gb, gr, s_f, s_b, tril, triu, e3, hmask, vmask, seg,
      jnp.tile(head_norm, GLA_HEADS).reshape(1, vw))


def _short_conv_kernel(x_ref, w_ref, b_ref, o_ref, u_ref, *, p, rows):
    n, c = x_ref.shape
    w = w_ref[...]
    bias = b_ref[...]
    rowi = lax.broadcasted_iota(jnp.int32, (rows, c), 0)
    zero = jnp.zeros((1, c), F32)
    for r0 in range(0, n, rows):
        cur = x_ref[r0:r0 + rows, :]
        before = x_ref[r0 - 1:r0, :] if r0 > 0 else zero
        after = x_ref[r0 + rows:r0 + rows + 1, :] if r0 + rows < n else zero
        prev = jnp.where(rowi == 0, before, pltpu.roll(cur, 1, axis=0))
        nxt = jnp.where(rowi == rows - 1, after, pltpu.roll(cur, rows - 1, axis=0))
        u = bias + prev * w[0:1] + cur * w[1:2] + nxt * w[2:3]
        for t in range(c // LANES):
            u_ref[t, r0:r0 + rows, :] = u[:, t * LANES:(t + 1) * LANES]
    for n1 in range(p):
        for t in range(c // LANES):
            col = n1 * c + t * LANES
            o_ref[:, col:col + LANES] = u_ref[t, pl.ds(n1, n // p, stride=p), :]


def _short_conv(hy_in, conv_w, conv_b, p, *, rows=512):
    b, s, c3 = hy_in.shape
    c = c3 // 3
    return pl.pallas_call(
        functools.partial(_short_conv_kernel, p=p, rows=rows),
        grid=(b, 3),
        in_specs=[
            pl.BlockSpec((None, s, c), lambda bi, j: (bi, 0, j)),
            pl.BlockSpec((HY_SHORT, c), lambda bi, j: (0, j)),
            pl.BlockSpec((1, c), lambda bi, j: (0, j)),
        ],
        out_specs=pl.BlockSpec((None, None, s // p, p * c), lambda bi, j: (j, bi, 0, 0)),
        out_shape=jax.ShapeDtypeStruct((3, b, s // p, p * c), F32),
        scratch_shapes=[pltpu.VMEM((c // LANES, s, LANES), F32)],
        compiler_params=_cparams("parallel", "parallel"),
        name="hy_short_conv",
    )(hy_in, conv_w, conv_b.reshape(1, c3))


def _filter_kernel(band_ref, w1_ref, b1_ref, f1_ref, w2_ref, b2_ref, f2_ref, w3_ref, delta_ref, o_ref,
                   *, seq):
    tr = o_ref.shape[1]
    c = delta_ref.shape[-1]
    n = pl.program_id(0) * tr + lax.broadcasted_iota(jnp.int32, (tr, LANES), 0)
    lane = lax.broadcasted_iota(jnp.int32, (tr, LANES), 1)
    pos = jnp.where(n < seq, n, 2 * seq - n).astype(F32)
    t = pos / (seq - 1)
    ang = (2.0 * math.pi * pos / seq) * band_ref[0:1, :] + band_ref[1:2, :]
    z = jnp.where(lane == 0, t, jnp.where(lane <= 2 * HY_BANDS, jnp.cos(ang), 0.0))
    h = jnp.sin(f1_ref[...] * (jnp.dot(z, w1_ref[...], precision=HIGHEST, preferred_element_type=F32)
                               + b1_ref[...]))
    h = jnp.sin(f2_ref[...] * (jnp.dot(h, w2_ref[...], precision=HIGHEST, preferred_element_type=F32)
                               + b2_ref[...]))
    h = jnp.dot(h, w3_ref[...], precision=HIGHEST, preferred_element_type=F32)
    t_c = jnp.concatenate([t] * (c // LANES), axis=-1)
    window = jnp.exp(-t_c * delta_ref[...]) + HY_DECAY_SHIFT
    live = jnp.concatenate([n] * (c // LANES), axis=-1) != seq
    for o in range(HY_ORDER):
        o_ref[o] = jnp.where(live, h[:, o * c:(o + 1) * c] * window, 0.0)


def _hyena_filters(seq, w1, b1, f1, w2, b2, f2, w3, *, tr=512):
    emb, hid = w1.shape
    c = w3.shape[1] // (HY_ORDER * 2)
    n = 2 * seq
    bands = np.zeros((2, LANES), np.float32)
    bands[1, 1 + HY_BANDS:1 + 2 * HY_BANDS] = 0.5 * np.pi
    bvals = np.linspace(1e-4, HY_BANDS - 1, HY_BANDS, dtype=np.float32)
    bands[0, 1:1 + HY_BANDS] = bvals
    bands[0, 1 + HY_BANDS:1 + 2 * HY_BANDS] = bvals
    max_decay = math.log(HY_DECAY_TARGET) / HY_FAST_DECAY_PCT
    min_decay = math.log(HY_DECAY_TARGET) / HY_SLOW_DECAY_PCT
    deltas = np.abs(np.linspace(min_decay, max_decay, c, dtype=np.float32)).reshape(1, c)
    w1p = jnp.pad(w1, ((0, LANES - emb), (0, 0)))
    w3d = w3.reshape(hid, HY_ORDER, 2, c).transpose(2, 0, 1, 3).reshape(2, hid, HY_ORDER * c)
    half_tiles = seq // tr
    kern = functools.partial(_filter_kernel, seq=seq)
    return pl.pallas_call(
        kern,
        grid=(n // tr,),
        in_specs=[
            _const_spec((2, LANES)),
            _const_spec((LANES, hid)), _const_spec((1, hid)), _const_spec((1, hid)),
            _const_spec((hid, hid)), _const_spec((1, hid)), _const_spec((1, hid)),
            pl.BlockSpec((None, hid, HY_ORDER * c), lambda i: (i // half_tiles, 0, 0)),
            _const_spec((1, c)),
        ],
        out_specs=pl.BlockSpec((HY_ORDER, tr, c), lambda i: (0, i, 0)),
        out_shape=jax.ShapeDtypeStruct((HY_ORDER, n, c), F32),
        compiler_params=_cparams("parallel"),
        name="hy_filter",
    )(jnp.asarray(bands), w1p, b1.reshape(1, hid), f1.reshape(1, hid), w2, b2.reshape(1, hid),
      f2.reshape(1, hid), w3d, jnp.asarray(deltas))


def _dft_tables(p):
    idx = np.arange(p)
    ang = 2.0 * np.pi * np.outer(idx, idx) / p
    fr, fi = np.cos(ang), -np.sin(ang)
    tang = 2.0 * np.pi * np.outer(idx, idx) / (p * p)
    twr, twi = np.cos(tang), -np.sin(tang)
    return fr, fi, twr, twi


def _split_bf16(x):
    hi = x.astype(BF16)
    return hi, (x - hi.astype(F32)).astype(BF16)


def _dft_dot(mat, data, *, split):
    dot = functools.partial(jnp.dot, preferred_element_type=F32)
    if not split:
        return dot(mat.astype(BF16), data.astype(BF16))
    mh, ml = _split_bf16(mat)
    dh, dl = _split_bf16(data)
    return dot(mh, dh) + (dot(mh, dl) + dot(ml, dh))


def _stage_a_kernel(x_ref, m_ref, o_ref, *, split):
    x = jnp.concatenate([x_ref[0], x_ref[1]], axis=0)
    o_ref[0] = _dft_dot(m_ref[...], x, split=split).astype(o_ref.dtype)


def _stage_a(x4, idx, mat, *, split, ct=2048):
    _, g2, hp, cols = x4.shape
    rows = mat.shape[0]
    return pl.pallas_call(
        functools.partial(_stage_a_kernel, split=split),
        grid=(g2 // 2, cols // ct),
        in_specs=[
            pl.BlockSpec((None, 2, hp, ct), lambda g, j: (idx, g, 0, j)),
            _const_spec(mat.shape),
        ],
        out_specs=pl.BlockSpec((1, rows, ct), lambda g, j: (g, 0, j)),
        out_shape=jax.ShapeDtypeStruct((g2 // 2, rows, cols), F32 if split else BF16),
        compiler_params=_cparams("parallel", "parallel"),
        name="hy_dft_a",
    )(x4, mat)


def _fwd_matrix(fr, fi, twr, twi):
    gr = fr * twr - fi * twi
    gi = fr * twi + fi * twr
    return jnp.concatenate([jnp.concatenate([gr, -gi], axis=1), jnp.concatenate([gi, gr], axis=1)], axis=0)


def _stack_pairs(a_ref, j):
    g = a_ref.shape[0]
    re = jnp.concatenate([a_ref[i, 0, j] for i in range(g)], axis=-1)
    im = jnp.concatenate([a_ref[i, 1, j] for i in range(g)], axis=-1)
    return jnp.concatenate([re, im], axis=0)


def _stage_c_kernel(a_ref, fr_ref, fi_ref, twr_ref, twi_ref, o_ref):
    g, _, kk, p, c = a_ref.shape
    fr, fi = fr_ref[...], fi_ref[...]
    for j in range(kk):
        gs = _fwd_matrix(fr, fi, twr_ref[j], twi_ref[j])
        x = _dft_dot(gs, _stack_pairs(a_ref, j), split=True)
        for i in range(g):
            o_ref[i, 0, j] = x[:p, i * c:(i + 1) * c]
            o_ref[i, 1, j] = x[p:, i * c:(i + 1) * c]


def _conv_core_kernel(a_ref, kh_ref, fr_ref, fi_ref, twr_ref, twi_ref, twbr_ref, twbi_ref, o_ref):
    g, _, kk, p, c = a_ref.shape
    fr, fi = fr_ref[...], fi_ref[...]
    for j in range(kk):
        gs = _fwd_matrix(fr, fi, twr_ref[j], twi_ref[j])
        x = _dft_dot(gs, _stack_pairs(a_ref, j), split=False)
        xr, xi = x[:p], x[p:]
        kr = jnp.concatenate([kh_ref[0, j]] * g, axis=-1)
        ki = jnp.concatenate([kh_ref[1, j]] * g, axis=-1)
        y = jnp.concatenate([xr * kr - xi * ki, xr * ki + xi * kr], axis=0)
        br = fr * twbr_ref[j] - fi * twbi_ref[j]
        bi = -(fr * twbi_ref[j] + fi * twbr_ref[j])
        bs = jnp.concatenate([jnp.concatenate([br, -bi], axis=1), jnp.concatenate([bi, br], axis=1)], axis=0)
        r = _dft_dot(bs, y, split=False).astype(o_ref.dtype)
        for i in range(g):
            o_ref[i, 0, j] = r[:p, i * c:(i + 1) * c]
            o_ref[i, 1, j] = r[p:, i * c:(i + 1) * c]


def _tw_specs(p, kk):
    row = pl.BlockSpec((kk, 1, p), lambda i: (i, 0, 0))
    return row


def _stage_c(a5, tabs, *, kk=4):
    g, _, p, _, c = a5.shape
    fr, fi, twr, twi = tabs
    blk = pl.BlockSpec((g, 2, kk, p, c), lambda i: (0, 0, i, 0, 0))
    return pl.pallas_call(
        _stage_c_kernel,
        grid=(p // kk,),
        in_specs=[blk, _const_spec((p, p)), _const_spec((p, p)), _tw_specs(p, kk), _tw_specs(p, kk)],
        out_specs=blk,
        out_shape=jax.ShapeDtypeStruct(a5.shape, F32),
        compiler_params=_cparams("parallel"),
        name="hy_dft_c",
    )(a5, fr, fi, twr.reshape(p, 1, p), twi.reshape(p, 1, p))


def _conv_core(a5, khat, order, tabs, twb, *, kk=8):
    g, _, p, _, c = a5.shape
    fr, fi, twr, twi = tabs
    twbr, twbi = twb
    blk = pl.BlockSpec((g, 2, kk, p, c), lambda i: (0, 0, i, 0, 0))
    return pl.pallas_call(
        _conv_core_kernel,
        grid=(p // kk,),
        in_specs=[
            blk,
            pl.BlockSpec((None, 2, kk, p, c), lambda i: (order, 0, i, 0, 0)),
            _const_spec((p, p)), _const_spec((p, p)), _tw_specs(p, kk), _tw_specs(p, kk),
            pl.BlockSpec((kk, p, p), lambda i: (i, 0, 0)),
            pl.BlockSpec((kk, p, p), lambda i: (i, 0, 0)),
        ],
        out_specs=blk,
        out_shape=jax.ShapeDtypeStruct(a5.shape, BF16),
        compiler_params=_cparams("parallel"),
        name="hy_conv_core",
    )(a5, khat, fr, fi, twr.reshape(p, 1, p), twi.reshape(p, 1, p), twbr, twbi)


def _stage_b_kernel(r_ref, m_ref, u_ref, gate_ref, skip_ref, o_ref):
    y = _dft_dot(m_ref[...], r_ref[0], split=False)
    hp = u_ref.shape[1]
    skip = skip_ref[...]
    o_ref[0] = gate_ref[0] * (y[:hp] + skip * u_ref[0])
    o_ref[1] = gate_ref[1] * (y[hp:] + skip * u_ref[1])


def _stage_b(r3, mat, u4, u_idx, gate4, gate_idx, skip_row, *, ct=2048):
    g, rows, cols = r3.shape
    hp = mat.shape[0] // 2
    nb = u4.shape[1]
    return pl.pallas_call(
        _stage_b_kernel,
        grid=(g, cols // ct),
        in_specs=[
            pl.BlockSpec((1, rows, ct), lambda gi, j: (gi, 0, j)),
            _const_spec(mat.shape),
            pl.BlockSpec((None, 2, hp, ct), lambda gi, j: (u_idx, gi, 0, j)),
            pl.BlockSpec((None, 2, hp, ct), lambda gi, j: (gate_idx, gi, 0, j)),
            pl.BlockSpec((1, ct), lambda gi, j: (0, j)),
        ],
        out_specs=pl.BlockSpec((2, hp, ct), lambda gi, j: (gi, 0, j)),
        out_shape=jax.ShapeDtypeStruct((nb, hp, cols), F32),
        compiler_params=_cparams("parallel", "parallel"),
        name="hy_dft_b",
    )(r3, mat, u4, gate4, skip_row)


def _hyena(hy_in, conv_w, conv_b, w1, b1, f1, w2, b2, f2, w3, skip):
    b, s, c3 = hy_in.shape
    c = c3 // 3
    p = math.isqrt(2 * s)
    assert p * p == 2 * s and b % 2 == 0
    hp = p // 2
    cols = p * c
    fr, fi, twr, twi = _dft_tables(p)
    scale = 1.0 / (p * p)
    mat_a = np.block([[fr[:, :hp], -fi[:, :hp]], [fi[:, :hp], fr[:, :hp]]]).astype(np.float32)
    mat_k = np.concatenate([fr, fi], axis=0).astype(np.float32)
    br, bi = fr[:hp] * scale, -fi[:hp] * scale
    mat_b = np.block([[br, -bi], [bi, br]]).astype(np.float32)
    tabs = tuple(jnp.asarray(t.astype(np.float32)) for t in (fr, fi, twr, twi))
    twb = tuple(jnp.broadcast_to(t[:, :, None], (p, p, p)) for t in tabs[2:])

    uvx = _short_conv(hy_in, conv_w, conv_b, p)
    kern = _hyena_filters(s, w1, b1, f1, w2, b2, f2, w3)
    ka = _stage_a(kern.reshape(1, HY_ORDER * 2, hp, cols), 0, jnp.asarray(mat_k), split=True)
    khat = _stage_c(ka.reshape(HY_ORDER, 2, p, p, c), tabs)

    skip_rows = jnp.tile(skip, (1, p))
    g = b // 2

    def long_conv(u4, u_idx, gate_idx, order):
        a = _stage_a(u4, u_idx, jnp.asarray(mat_a), split=False)
        r = _conv_core(a.reshape(g, 2, p, p, c), khat, order, tabs, twb)
        return _stage_b(r.reshape(g, 2 * p, cols), jnp.asarray(mat_b), u4, u_idx, uvx, gate_idx,
                        skip_rows[order:order + 1])

    z = long_conv(uvx, 0, 1, 0)
    y = long_conv(z[None], 0, 2, 1)
    return y.reshape(b, s, c)


def kernel(x, ffn1_norm, ffn1_w_gate, ffn1_w_up, ffn1_w_down, mix_norm, w_in, hy_conv_w, hy_conv_b, hy_filt_w1, hy_filt_b1, hy_filt_freq1, hy_filt_w2, hy_filt_b2, hy_filt_freq2, hy_filt_w3, hy_skip, hy_out_norm, mla_q_norm, mla_w_uq, mla_kv_norm, mla_w_ukv, mla_out_norm, gla_w_gate_fwd, gla_b_gate_fwd, gla_w_gate_bwd, gla_b_gate_bwd, gla_head_norm, w_out, ffn2_norm, ffn2_w_gate, ffn2_w_up, ffn2_w_down, final_norm):
    b, s, d = x.shape
    depth = ffn1_norm.shape[0]
    t = b * s
    for l in range(depth):
        x2 = _ffn(x.reshape(t, d), ffn1_norm[l], ffn1_w_gate[l], ffn1_w_up[l], ffn1_w_down[l], final_norm,
                  final=False)
        (hy_in, q, k, v, gq, gk, gv, gf, gb, gr) = _in_proj(
            x2.reshape(b, s, d), mix_norm[l], w_in[l], mla_q_norm[l], mla_w_uq[l], mla_kv_norm[l],
            mla_w_ukv[l], gla_w_gate_fwd[l], gla_b_gate_fwd[l], gla_w_gate_bwd[l], gla_b_gate_bwd[l])
        y_hy = _hyena(hy_in, hy_conv_w[l], hy_conv_b[l], hy_filt_w1[l], hy_filt_b1[l], hy_filt_freq1[l],
                      hy_filt_w2[l], hy_filt_b2[l], hy_filt_freq2[l], hy_filt_w3[l], hy_skip[l])
        o_mla = _mla_attention(q, k, v)
        y_gla = _gla(gq, gk, gv, gf, gb, gr, gla_head_norm[l])
        mix = (y_hy.reshape(t, -1), o_mla.reshape(t, -1), y_gla.reshape(t, -1), hy_out_norm[l],
               mla_out_norm[l], w_out[l])
        x = _ffn(x2, ffn2_norm[l], ffn2_w_gate[l], ffn2_w_up[l], ffn2_w_down[l], final_norm,
                 final=(l == depth - 1), mix=mix).reshape(b, s, d)
    return x
```

```python
import functools
import math

import jax
import jax.numpy as jnp
import numpy as np
from jax import lax
from jax.experimental import pallas as pl
from jax.experimental.pallas import tpu as pltpu

F32 = jnp.float32
BF16 = jnp.bfloat16
HIGHEST = lax.Precision.HIGHEST

NORM_EPS = 1e-6
HY_ORDER = 2
HY_SHORT = 3
HY_BANDS = 16
HY_FAST_DECAY_PCT = 0.3
HY_SLOW_DECAY_PCT = 1.5
HY_DECAY_TARGET = 1e-2
HY_DECAY_SHIFT = 0.0
MLA_HEADS = 4
MLA_QK_NOPE = 128
MLA_QK_ROPE = 64
MLA_V_HEAD = 128
ROPE_BASE = 10000.0
GLA_HEADS = 4
GLA_GATE_RANK = 16
GLA_GATE_NORM = 16.0

LANES = 128
GLA_CHUNK = 64
GLA_SUB = 16
GLA_FACTORED_MIN_GATE = -1.0
VMEM_LIMIT = 56 * 1024 * 1024

NEG_BIG = -1e30


def _cparams(*sem):
    return pltpu.CompilerParams(dimension_semantics=sem, vmem_limit_bytes=VMEM_LIMIT)


def _rms(x, g):
    ms = jnp.mean(x * x, axis=-1, keepdims=True)
    return x * lax.rsqrt(ms + NORM_EPS) * g


def _const_spec(shape):
    nd = len(shape)
    return pl.BlockSpec(shape, lambda *_: (0,) * nd)


def _ffn_kernel(x_ref, g_ref, wg_ref, wu_ref, wd_ref, fg_ref, o_ref, *, ff_chunk, final):
    _ffn_body(x_ref[...], g_ref, wg_ref, wu_ref, wd_ref, fg_ref, o_ref, ff_chunk=ff_chunk, final=final)


def _mix_ffn_kernel(x_ref, hy_ref, mla_ref, gla_ref, ghy_ref, gmla_ref, wout_ref,
                    g_ref, wg_ref, wu_ref, wd_ref, fg_ref, o_ref, *, ff_chunk, final):
    y_hy = _rms(hy_ref[...], ghy_ref[...]).astype(BF16)
    y_mla = _rms(mla_ref[...], gmla_ref[...]).astype(BF16)
    cat = jnp.concatenate([y_hy, y_mla, gla_ref[...]], axis=-1)
    x = x_ref[...] + jnp.dot(cat, wout_ref[...], preferred_element_type=F32)
    _ffn_body(x, g_ref, wg_ref, wu_ref, wd_ref, fg_ref, o_ref, ff_chunk=ff_chunk, final=final)


def _ffn_body(x, g_ref, wg_ref, wu_ref, wd_ref, fg_ref, o_ref, *, ff_chunk, final):
    xn = _rms(x, g_ref[...]).astype(BF16)
    d_ff = wg_ref.shape[1]
    acc = jnp.zeros_like(x)
    for c in range(0, d_ff, ff_chunk):
        gate = jnp.dot(xn, wg_ref[:, c:c + ff_chunk], preferred_element_type=F32)
        up = jnp.dot(xn, wu_ref[:, c:c + ff_chunk], preferred_element_type=F32)
        h = (gate * jax.nn.sigmoid(gate) * up).astype(BF16)
        acc = acc + jnp.dot(h, wd_ref[c:c + ff_chunk, :], preferred_element_type=F32)
    y = x + 0.5 * acc
    if final:
        y = _rms(y, fg_ref[...])
    o_ref[...] = y


def _resident_spec(shape):
    nd = len(shape)
    return pl.BlockSpec(shape, lambda *_: (0,) * nd, pipeline_mode=pl.Buffered(1))


def _ffn(x2d, norm_g, w_gate, w_up, w_down, final_g, *, final, mix=None, tm=512, ff_chunk=256):
    t, d = x2d.shape
    d_ff = w_gate.shape[1]
    tok = lambda w: pl.BlockSpec((tm, w), lambda i: (i, 0))
    ffn_specs = [_const_spec((1, d)), _resident_spec((d, d_ff)), _resident_spec((d, d_ff)),
                 _resident_spec((d_ff, d)), _const_spec((1, d))]
    ffn_args = (norm_g.reshape(1, d), w_gate.astype(BF16), w_up.astype(BF16), w_down.astype(BF16),
                final_g.reshape(1, d))
    if mix is None:
        kern, name = _ffn_kernel, "ffn"
        in_specs = [tok(d)] + ffn_specs
        args = (x2d,) + ffn_args
    else:
        y_hy, o_mla, y_gla, g_hy, g_mla, w_out = mix
        kern, name = _mix_ffn_kernel, "mix_ffn"
        in_specs = [tok(d), tok(y_hy.shape[1]), tok(o_mla.shape[1]), tok(y_gla.shape[1]),
                    _const_spec((1, y_hy.shape[1])), _const_spec((1, o_mla.shape[1])),
                    _resident_spec(w_out.shape)] + ffn_specs
        args = (x2d, y_hy, o_mla, y_gla, g_hy.reshape(1, -1), g_mla.reshape(1, -1),
                w_out.astype(BF16)) + ffn_args
    return pl.pallas_call(
        functools.partial(kern, ff_chunk=ff_chunk, final=final),
        grid=(t // tm,),
        in_specs=in_specs,
        out_specs=tok(d),
        out_shape=jax.ShapeDtypeStruct((t, d), F32),
        compiler_params=_cparams("parallel"),
        name=name,
    )(*args)


def _log_sigmoid(z):
    return jnp.minimum(z, 0.0) - jnp.log(1.0 + jnp.exp(-jnp.abs(z)))


def _in_proj_kernel(x_ref, g_ref, win_ref, qn_ref, wuq_ref, kvn_ref, wukv_ref, cos_ref, sin_ref,
                    wgate_ref, bgate_ref,
                    hy_ref, q_ref, k_ref, v_ref, gq_ref, gk_ref, gv_ref, gf_ref, gb_ref, gr_ref,
                    *, off, q_scale, gla_q_scale):
    x = x_ref[0]
    xn = _rms(x, g_ref[...]).astype(BF16)
    proj = jnp.dot(xn, win_ref[...], preferred_element_type=F32)

    def blk(name):
        lo, hi = off[name]
        return proj[:, lo:hi]

    hy_ref[0] = blk("hy")

    qn = _rms(blk("cq"), qn_ref[...]).astype(BF16)
    qf = jnp.dot(qn, wuq_ref[...], preferred_element_type=F32)
    kvn = _rms(blk("ckv"), kvn_ref[...]).astype(BF16)
    kvf = jnp.dot(kvn, wukv_ref[...], preferred_element_type=F32)
    cos = cos_ref[...]
    sin = sin_ref[...]
    k_rope = blk("kr") * cos + blk("kr_rot") * sin
    ones = jnp.ones_like(k_rope)
    qw = 3 * LANES
    kvw = 2 * LANES
    for h in range(MLA_HEADS):
        q_nope = qf[:, h * qw:h * qw + LANES]
        q_rope = (qf[:, h * qw + LANES:h * qw + 2 * LANES] * cos
                  + qf[:, h * qw + 2 * LANES:(h + 1) * qw] * sin)
        q_ref[0, h] = (jnp.concatenate([q_nope, q_rope], axis=-1) * q_scale).astype(BF16)
        k_ref[0, h] = jnp.concatenate([kvf[:, h * kvw:h * kvw + LANES], k_rope], axis=-1).astype(BF16)
        v_ref[0, h] = jnp.concatenate([kvf[:, h * kvw + LANES:(h + 1) * kvw], ones], axis=-1).astype(BF16)

    gq_ref[0] = blk("gq") * gla_q_scale
    gk_ref[0] = blk("gk")
    gv_ref[0] = blk("gv")
    z = jnp.dot(blk("gate"), wgate_ref[...], precision=HIGHEST, preferred_element_type=F32) + bgate_ref[...]
    gates = _log_sigmoid(z) / GLA_GATE_NORM
    gk_w = gf_ref.shape[-1]
    gf_ref[0] = gates[:, :gk_w]
    gb_ref[0] = gates[:, gk_w:]
    gr_ref[0] = blk("gr")


def _rope_rot_cols(w):
    half = w.shape[-1] // 2
    return jnp.concatenate([-w[..., half:], w[..., :half]], axis=-1)


def _pad_cols(w, width):
    return jnp.pad(w, ((0, 0), (0, width - w.shape[-1])))


def _in_proj(x, mix_g, w_in, q_norm, w_uq, kv_norm, w_ukv, w_gf, b_gf, w_gb, b_gb, *, tm=512):
    b, s, d = x.shape
    hy_in = 3 * (d // 4)
    q_lora = d // 4
    kv_lora = d // 8
    gla_w = d // 4
    gla_key = gla_w // 2
    rope = MLA_QK_ROPE
    splits = (hy_in, q_lora, kv_lora, rope, gla_key, gla_key, gla_w, GLA_GATE_RANK, GLA_GATE_RANK, gla_w)
    offs = np.concatenate([[0], np.cumsum(splits)])
    (w_hy, w_cq, w_ckv, w_kr, w_gq, w_gk, w_gv, w_gfl, w_gbl, w_gr) = [
        w_in[:, offs[i]:offs[i + 1]] for i in range(len(splits))]
    pieces = [
        ("hy", w_hy), ("cq", w_cq), ("ckv", w_ckv),
        ("kr", _pad_cols(w_kr, LANES)), ("kr_rot", _pad_cols(_rope_rot_cols(w_kr), LANES)),
        ("gq", w_gq), ("gk", w_gk), ("gv", w_gv),
        ("gate", _pad_cols(jnp.concatenate([w_gfl, w_gbl], axis=1), LANES)),
        ("gr", w_gr),
    ]
    off = {}
    pos = 0
    for name, w in pieces:
        off[name] = (pos, pos + w.shape[1])
        pos += w.shape[1]
    win_r = jnp.concatenate([w for _, w in pieces], axis=1).astype(BF16)
    n_in = win_r.shape[1]

    dqk = MLA_QK_NOPE + MLA_QK_ROPE
    wq = w_uq.reshape(q_lora, MLA_HEADS, dqk)
    wq_nope = wq[..., :MLA_QK_NOPE]
    wq_rope = wq[..., MLA_QK_NOPE:]
    zpad = jnp.zeros((q_lora, MLA_HEADS, LANES - rope), F32)
    wuq_r = jnp.concatenate([wq_nope, wq_rope, zpad, _rope_rot_cols(wq_rope), zpad], axis=-1)
    wuq_r = wuq_r.reshape(q_lora, MLA_HEADS * 3 * LANES).astype(BF16)
    wukv_r = w_ukv.astype(BF16)

    half = rope // 2
    inv = ROPE_BASE ** (-jnp.arange(half, dtype=F32) * 2.0 / rope)
    ang = jnp.arange(s, dtype=F32)[:, None] * inv[None, :]
    zt = jnp.zeros((s, LANES - rope), F32)
    cos_t = jnp.concatenate([jnp.cos(ang), jnp.cos(ang), zt], axis=1)
    sin_t = jnp.concatenate([jnp.sin(ang), jnp.sin(ang), zt], axis=1)

    wgate = jnp.zeros((LANES, 2 * gla_key), F32)
    wgate = wgate.at[:GLA_GATE_RANK, :gla_key].set(w_gf)
    wgate = wgate.at[GLA_GATE_RANK:2 * GLA_GATE_RANK, gla_key:].set(w_gb)
    bgate = jnp.concatenate([b_gf, b_gb]).reshape(1, 2 * gla_key)

    kern = functools.partial(_in_proj_kernel, off=off, q_scale=float(dqk) ** -0.5 * math.log2(math.e),
                             gla_q_scale=float(gla_key // GLA_HEADS) ** -0.5)
    tok = lambda w: pl.BlockSpec((1, tm, w), lambda bi, i: (bi, i, 0))
    head = lambda w: pl.BlockSpec((1, MLA_HEADS, tm, w), lambda bi, i: (bi, 0, i, 0))
    outs = pl.pallas_call(
        kern,
        grid=(b, s // tm),
        in_specs=[
            tok(d),
            _const_spec((1, d)),
            _const_spec((d, n_in)),
            _const_spec((1, q_lora)),
            _const_spec((q_lora, MLA_HEADS * 3 * LANES)),
            _const_spec((1, kv_lora)),
            _const_spec((kv_lora, MLA_HEADS * 2 * LANES)),
            pl.BlockSpec((tm, LANES), lambda bi, i: (i, 0)),
            pl.BlockSpec((tm, LANES), lambda bi, i: (i, 0)),
            _const_spec((LANES, 2 * gla_key)),
            _const_spec((1, 2 * gla_key)),
        ],
        out_specs=[
            tok(hy_in),
            head(2 * LANES), head(2 * LANES), head(2 * LANES),
            tok(gla_key), tok(gla_key), tok(gla_w), tok(gla_key), tok(gla_key), tok(gla_w),
        ],
        out_shape=[
            jax.ShapeDtypeStruct((b, s, hy_in), F32),
            jax.ShapeDtypeStruct((b, MLA_HEADS, s, 2 * LANES), BF16),
            jax.ShapeDtypeStruct((b, MLA_HEADS, s, 2 * LANES), BF16),
            jax.ShapeDtypeStruct((b, MLA_HEADS, s, 2 * LANES), BF16),
            jax.ShapeDtypeStruct((b, s, gla_key), F32),
            jax.ShapeDtypeStruct((b, s, gla_key), F32),
            jax.ShapeDtypeStruct((b, s, gla_w), F32),
            jax.ShapeDtypeStruct((b, s, gla_key), F32),
            jax.ShapeDtypeStruct((b, s, gla_key), F32),
            jax.ShapeDtypeStruct((b, s, gla_w), F32),
        ],
        compiler_params=_cparams("parallel", "parallel"),
        name="in_proj",
    )(x, mix_g.reshape(1, d), win_r, q_norm.reshape(1, -1), wuq_r, kv_norm.reshape(1, -1), wukv_r,
      cos_t, sin_t, wgate, bgate)
    return outs


def _attn_kernel(q_ref, k_ref, v_ref, o_ref, s0_ref, s1_ref, p_ref, acc_ref, m_ref, alpha_ref, *, tk, rb):
    q = q_ref[...]
    tq = q.shape[0]
    nk = k_ref.shape[0] // tk
    dv = o_ref.shape[-1]

    def scores(j, dst_ref):
        start = pl.multiple_of(j * tk, tk)
        dst_ref[...] = lax.dot_general(q, k_ref[pl.ds(start, tk), :], (((1,), (1,)), ((), ())),
                                       preferred_element_type=F32)

    def consume(j, cur_ref, nxt_ref):
        if nxt_ref is not None:
            scores(j + 1, nxt_ref)
        for r in range(tq // rb):
            rows = slice(r * rb, (r + 1) * rb)
            tiles = [cur_ref[rows, t * LANES:(t + 1) * LANES] for t in range(tk // LANES)]
            mx = functools.reduce(jnp.maximum, tiles)
            m_old = m_ref[rows, :]
            m_new = jnp.maximum(m_old, jnp.broadcast_to(jnp.max(mx, axis=-1, keepdims=True), m_old.shape))
            alpha_ref[rows, :] = jnp.exp2(m_old - m_new)
            for t, s_t in enumerate(tiles):
                p_ref[rows, t * LANES:(t + 1) * LANES] = jnp.exp2(s_t - m_new).astype(BF16)
            m_ref[rows, :] = m_new
        start = pl.multiple_of(j * tk, tk)
        pv = jnp.dot(p_ref[...], v_ref[pl.ds(start, tk), :], preferred_element_type=F32)
        alpha = alpha_ref[...]
        for t in range(acc_ref.shape[1] // LANES):
            cols = slice(t * LANES, (t + 1) * LANES)
            acc_ref[:, cols] = acc_ref[:, cols] * alpha + pv[:, cols]

    m_ref[...] = jnp.full(m_ref.shape, -jnp.inf, F32)
    acc_ref[...] = jnp.zeros(acc_ref.shape, F32)
    scores(0, s0_ref)

    def pair(i, carry):
        consume(2 * i, s0_ref, s1_ref)
        consume(2 * i + 1, s1_ref, s0_ref)
        return carry

    lax.fori_loop(0, nk // 2 - 1, pair, 0)
    consume(nk - 2, s0_ref, s1_ref)
    consume(nk - 1, s1_ref, None)
    acc = acc_ref[...]
    o_ref[...] = acc[:, :dv] / acc[:, dv:]


def _mla_attention(q, k, v, *, tq=1024, tk=2048, rb=64):
    b, h, s, dq = q.shape
    dv = MLA_V_HEAD
    tk = min(tk, s // 2)
    assert (s // tk) % 2 == 0
    kern = functools.partial(_attn_kernel, tk=tk, rb=rb)
    return pl.pallas_call(
        kern,
        grid=(b, h, s // tq),
        in_specs=[
            pl.BlockSpec((None, None, tq, dq), lambda bi, hi, i: (bi, hi, i, 0)),
            pl.BlockSpec((None, None, s, dq), lambda bi, hi, i: (bi, hi, 0, 0)),
            pl.BlockSpec((None, None, s, 2 * dv), lambda bi, hi, i: (bi, hi, 0, 0)),
        ],
        out_specs=pl.BlockSpec((None, tq, dv), lambda bi, hi, i: (bi, i, hi)),
        out_shape=jax.ShapeDtypeStruct((b, s, h * dv), F32),
        scratch_shapes=[
            pltpu.VMEM((tq, tk), F32), pltpu.VMEM((tq, tk), F32), pltpu.VMEM((tq, tk), BF16),
            pltpu.VMEM((tq, 2 * dv), F32), pltpu.VMEM((tq, LANES), F32), pltpu.VMEM((tq, LANES), F32),
        ],
        compiler_params=_cparams("parallel", "parallel", "parallel"),
        name="mla_attn",
    )(q, k, v)


def _split3(x):
    x0 = x.astype(BF16)
    r1 = x - x0.astype(F32)
    x1 = r1.astype(BF16)
    x2 = (r1 - x1.astype(F32)).astype(BF16)
    return x0, x1, x2


def _chunk_cumsum(tri, g):
    g0, g1, g2 = _split3(g)
    dot = functools.partial(jnp.dot, preferred_element_type=F32)
    return dot(tri, g0) + (dot(tri, g1) + dot(tri, g2))


def _gla_intra(q, k, v, b2f, b2b, e3, hmask, vmask):
    c = q.shape[0]
    jj = lax.broadcasted_iota(jnp.int32, (GLA_SUB, GLA_SUB, LANES), 0)
    ii = lax.broadcasted_iota(jnp.int32, (GLA_SUB, GLA_SUB, LANES), 1)
    lower = jj <= ii
    row = lax.broadcasted_iota(jnp.int32, (c, LANES), 0)
    v16 = v.astype(BF16)
    outs = []
    for sb in range(c // GLA_SUB):
        r0, r1 = sb * GLA_SUB, (sb + 1) * GLA_SUB
        qs, ks, vs, bf, bb = q[r0:r1], k[r0:r1], v[r0:r1], b2f[r0:r1], b2b[r0:r1]
        d = jnp.where(lower, bf[None, :, :] - bf[:, None, :], bb[None, :, :] - bb[:, None, :])
        p = (qs[None, :, :] * ks[:, None, :]) * jnp.exp2(d)
        w = jnp.dot(p.reshape(GLA_SUB * GLA_SUB, LANES).astype(BF16), e3, preferred_element_type=F32)
        o_sb = jnp.sum(w.reshape(GLA_SUB, GLA_SUB, w.shape[-1]) * vs[:, None, :], axis=0)
        q_parts, k_parts = [], []
        if r0 > 0:
            ref = b2f[r0 - 1:r0]
            q_parts.append(qs * jnp.exp2(bf - ref))
            k_parts.append(k * jnp.exp2(jnp.where(row < r0, ref - b2f, NEG_BIG)))
        if r1 < c:
            ref = b2b[r1:r1 + 1]
            q_parts.append(qs * jnp.exp2(bb - ref))
            k_parts.append(k * jnp.exp2(jnp.where(row >= r1, ref - b2b, NEG_BIG)))
        qt = jnp.concatenate(q_parts, axis=-1)
        kt = jnp.concatenate(k_parts, axis=-1).astype(BF16)
        hm = jnp.concatenate([hmask] * len(q_parts), axis=-1)
        q_stack = jnp.concatenate([qt * hm[hh:hh + 1] for hh in range(GLA_HEADS)], axis=0).astype(BF16)
        a = lax.dot_general(q_stack, kt, (((1,), (1,)), ((), ())), preferred_element_type=F32)
        o_all = jnp.dot(a.astype(BF16), v16, preferred_element_type=F32)
        for hh in range(GLA_HEADS):
            o_sb = o_sb + o_all[hh * GLA_SUB:(hh + 1) * GLA_SUB] * vmask[hh:hh + 1]
        outs.append(o_sb)
    return jnp.concatenate(outs, axis=0)


def _gla_state_kernel(kf_ref, vf_ref, gf_ref, kb_ref, vb_ref, gb_ref, tril_ref, triu_ref, stmask_ref,
                      sf_ref, sb_ref, stf_ref, stb_ref):
    @pl.when(pl.program_id(1) == 0)
    def _():
        stf_ref[...] = jnp.zeros_like(stf_ref)
        stb_ref[...] = jnp.zeros_like(stb_ref)

    nc = kf_ref.shape[0] // GLA_CHUNK
    stmask = stmask_ref[...]
    chunks = [slice(c * GLA_CHUNK, (c + 1) * GLA_CHUNK) for c in range(nc)]
    tn = (((0,), (0,)), ((), ()))

    for reverse in (False, True):
        if reverse:
            k_ref, v_ref, g_ref, s_ref, st_ref, tri = kb_ref, vb_ref, gb_ref, sb_ref, stb_ref, triu_ref[...]
        else:
            k_ref, v_ref, g_ref, s_ref, st_ref, tri = kf_ref, vf_ref, gf_ref, sf_ref, stf_ref, tril_ref[...]
        g, k, v16 = g_ref[...], k_ref[...], v_ref[...].astype(BF16)
        beta = [_chunk_cumsum(tri, g[r]) for r in chunks]
        tot = [b[0:1] if reverse else b[GLA_CHUNK - 1:GLA_CHUNK] for b in beta]
        kd = [(k[r] * jnp.exp(t - b)).astype(BF16) for r, t, b in zip(chunks, tot, beta)]
        ut = [lax.dot_general(v16[r], x, tn, preferred_element_type=F32) for r, x in zip(chunks, kd)]
        dec = [jnp.exp(t) for t in tot]
        st = st_ref[...]
        for c in (reversed(range(nc)) if reverse else range(nc)):
            s_ref[c] = st.astype(BF16)
            st = st * dec[c] + ut[c] * stmask
        st_ref[...] = st


def _gla_intra_factored(qf, qb, k, v16, b2f, b2b, hmask, vmask, lower):
    nt = (((1,), (1,)), ((), ()))
    n = len(k)
    c = k[0].shape[0]

    def head_rows(x):
        return jnp.concatenate([x * hmask[hh:hh + 1] for hh in range(GLA_HEADS)], axis=0).astype(BF16)

    kf = [(k[i] * jnp.exp2(-b2f[i])).astype(BF16) for i in range(n)]
    kb = [(k[i] * jnp.exp2(-b2b[i])).astype(BF16) for i in range(n)]
    hf = [head_rows(x) for x in qf]
    hb = [head_rows(x) for x in qb]
    a_f = [lax.dot_general(hf[i], kf[i], nt, preferred_element_type=F32) for i in range(n)]
    a_b = [lax.dot_general(hb[i], kb[i], nt, preferred_element_type=F32) for i in range(n)]
    a = [jnp.where(lower, a_f[i], a_b[i]).astype(BF16) for i in range(n)]
    o_all = [jnp.dot(a[i], v16[i], preferred_element_type=F32) for i in range(n)]
    outs = []
    for i in range(n):
        o = o_all[i][0:c] * vmask[0:1]
        for hh in range(1, GLA_HEADS):
            o = o + o_all[i][hh * c:(hh + 1) * c] * vmask[hh:hh + 1]
        outs.append(o)
    return outs


def _gla_out_kernel(q_ref, k_ref, v_ref, gf_ref, gb_ref, r_ref, sf_ref, sb_ref,
                    tril_ref, triu_ref, e3_ref, hmask_ref, vmask_ref, seg_ref, gn_ref, o_ref):
    nc = q_ref.shape[0] // GLA_CHUNK
    tril, triu = tril_ref[...], triu_ref[...]
    e3, hmask, vmask = e3_ref[...], hmask_ref[...], vmask_ref[...]
    seg, gn = seg_ref[...], gn_ref[...]
    log2e = math.log2(math.e)
    ri = lax.broadcasted_iota(jnp.int32, (GLA_HEADS * GLA_CHUNK, GLA_CHUNK), 0) % GLA_CHUNK
    cj = lax.broadcasted_iota(jnp.int32, (GLA_HEADS * GLA_CHUNK, GLA_CHUNK), 1)
    lower = cj <= ri

    dot = functools.partial(jnp.dot, preferred_element_type=F32)
    nt = (((1,), (1,)), ((), ()))

    def finish(o, r):
        s0, s1, s2 = _split3(o * o)
        ms = dot(s0, seg) + (dot(s1, seg) + dot(s2, seg))
        return (o * lax.rsqrt(ms + NORM_EPS) * gn * (r * jax.nn.sigmoid(r))).astype(o_ref.dtype)

    def elementwise_step(c, carry):
        rows = pl.ds(pl.multiple_of(c * GLA_CHUNK, GLA_CHUNK), GLA_CHUNK)
        q, k, v = q_ref[rows, :], k_ref[rows, :], v_ref[rows, :]
        b2f = _chunk_cumsum(tril, gf_ref[rows, :] * log2e)
        b2b = _chunk_cumsum(triu, gb_ref[rows, :] * log2e)
        qd = jnp.concatenate([q * jnp.exp2(b2f), q * jnp.exp2(b2b)], axis=-1).astype(BF16)
        st = jnp.concatenate([sf_ref[c], sb_ref[c]], axis=-1)
        o = lax.dot_general(qd, st, nt, preferred_element_type=F32)
        o = o + _gla_intra(q, k, v, b2f, b2b, e3, hmask, vmask)
        o_ref[rows, :] = finish(o, r_ref[rows, :])
        return carry

    def factored_block():
        chunks = [slice(c * GLA_CHUNK, (c + 1) * GLA_CHUNK) for c in range(nc)]
        g2f, g2b = gf_ref[...] * log2e, gb_ref[...] * log2e
        q, k, v16 = q_ref[...], k_ref[...], v_ref[...].astype(BF16)
        b2f = [_chunk_cumsum(tril, g2f[r]) for r in chunks]
        b2b = [_chunk_cumsum(triu, g2b[r]) for r in chunks]
        qf = [q[r] * jnp.exp2(b) for r, b in zip(chunks, b2f)]
        qb = [q[r] * jnp.exp2(b) for r, b in zip(chunks, b2b)]
        inter = [lax.dot_general(jnp.concatenate([qf[c], qb[c]], axis=-1).astype(BF16),
                                 jnp.concatenate([sf_ref[c], sb_ref[c]], axis=-1), nt,
                                 preferred_element_type=F32) for c in range(nc)]
        intra = _gla_intra_factored(qf, qb, [k[r] for r in chunks], [v16[r] for r in chunks], b2f, b2b,
                                    hmask, vmask, lower)
        o = jnp.concatenate([a + b for a, b in zip(inter, intra)], axis=0)
        o_ref[...] = finish(o, r_ref[...])

    gmin = jnp.minimum(jnp.min(gf_ref[...]), jnp.min(gb_ref[...]))
    mild = gmin >= GLA_FACTORED_MIN_GATE

    @pl.when(mild)
    def _():
        factored_block()

    @pl.when(jnp.logical_not(mild))
    def _():
        lax.fori_loop(0, nc, elementwise_step, 0, unroll=4)


def _gla(gq, gk, gv, gf, gb, gr, head_norm, *, tb=512):
    b, s, kw = gq.shape
    vw = gv.shape[-1]
    nb = s // tb
    ncb = tb // GLA_CHUNK
    dk = kw // GLA_HEADS
    dv = vw // GLA_HEADS
    idx = np.arange(GLA_CHUNK)
    tril = jnp.asarray((idx[None, :] <= idx[:, None]).astype(np.float32)).astype(BF16)
    triu = jnp.asarray((idx[None, :] >= idx[:, None]).astype(np.float32)).astype(BF16)
    d_head = np.arange(kw) // dk
    e_head = np.arange(vw) // dv
    e3 = jnp.asarray((d_head[:, None] == e_head[None, :]).astype(np.float32)).astype(BF16)
    hmask = jnp.asarray((np.arange(GLA_HEADS)[:, None] == d_head[None, :]).astype(np.float32))
    vmask = jnp.asarray((np.arange(GLA_HEADS)[:, None] == e_head[None, :]).astype(np.float32))
    stmask = jnp.asarray((e_head[:, None] == d_head[None, :]).astype(np.float32))
    seg = jnp.asarray((e_head[:, None] == e_head[None, :]).astype(np.float32) / dv).astype(BF16)
    tri_spec = _const_spec((GLA_CHUNK, GLA_CHUNK))

    fwd = lambda w: pl.BlockSpec((None, tb, w), lambda bi, i: (bi, i, 0))
    bwd = lambda w: pl.BlockSpec((None, tb, w), lambda bi, i: (bi, nb - 1 - i, 0))
    st_shape = jax.ShapeDtypeStruct((b, s // GLA_CHUNK, vw, kw), BF16)
    s_f, s_b = pl.pallas_call(
        _gla_state_kernel,
        grid=(b, nb),
        in_specs=[fwd(kw), fwd(vw), fwd(kw), bwd(kw), bwd(vw), bwd(kw), tri_spec, tri_spec,
                  _const_spec((vw, kw))],
        out_specs=[pl.BlockSpec((None, ncb, vw, kw), lambda bi, i: (bi, i, 0, 0)),
                   pl.BlockSpec((None, ncb, vw, kw), lambda bi, i: (bi, nb - 1 - i, 0, 0))],
        out_shape=[st_shape, st_shape],
        scratch_shapes=[pltpu.VMEM((vw, kw), F32), pltpu.VMEM((vw, kw), F32)],
        compiler_params=_cparams("parallel", "arbitrary"),
        name="gla_state",
    )(gk, gv, gf, gk, gv, gb, tril, triu, stmask)

    st_spec = pl.BlockSpec((None, ncb, vw, kw), lambda bi, i: (bi, i, 0, 0))
    return pl.pallas_call(
        _gla_out_kernel,
        grid=(b, nb),
        in_specs=[fwd(kw), fwd(kw), fwd(vw), fwd(kw), fwd(kw), fwd(vw), st_spec, st_spec,
                  tri_spec, tri_spec, _const_spec((kw, vw)), _const_spec((GLA_HEADS, kw)),
                  _const_spec((GLA_HEADS, vw)), _const_spec((vw, vw)), _const_spec((1, vw))],
        out_specs=fwd(vw),
        out_shape=jax.ShapeDtypeStruct((b, s, vw), BF16),
        compiler_params=_cparams("parallel", "parallel"),
        name="gla_out",
    )(gq, gk, gv, gf, gb, gr, s_f, s_b, tril, triu, e3, hmask, vmask, seg,
      jnp.tile(head_norm, GLA_HEADS).reshape(1, vw))


def _short_conv_kernel(x_ref, w_ref, b_ref, o_ref, u_ref, *, p, rows):
    n, c = x_ref.shape
    w = w_ref[...]
    bias = b_ref[...]
    rowi = lax.broadcasted_iota(jnp.int32, (rows, c), 0)
    zero = jnp.zeros((1, c), F32)
    for r0 in range(0, n, rows):
        cur = x_ref[r0:r0 + rows, :]
        before = x_ref[r0 - 1:r0, :] if r0 > 0 else zero
        after = x_ref[r0 + rows:r0 + rows + 1, :] if r0 + rows < n else zero
        prev = jnp.where(rowi == 0, before, pltpu.roll(cur, 1, axis=0))
        nxt = jnp.where(rowi == rows - 1, after, pltpu.roll(cur, rows - 1, axis=0))
        u = bias + prev * w[0:1] + cur * w[1:2] + nxt * w[2:3]
        for t in range(c // LANES):
            u_ref[t, r0:r0 + rows, :] = u[:, t * LANES:(t + 1) * LANES]
    for n1 in range(p):
        for t in range(c // LANES):
            col = n1 * c + t * LANES
            o_ref[:, col:col + LANES] = u_ref[t, pl.ds(n1, n // p, stride=p), :]


def _short_conv(hy_in, conv_w, conv_b, p, *, rows=512):
    b, s, c3 = hy_in.shape
    c = c3 // 3
    return pl.pallas_call(
        functools.partial(_short_conv_kernel, p=p, rows=rows),
        grid=(b, 3),
        in_specs=[
            pl.BlockSpec((None, s, c), lambda bi, j: (bi, 0, j)),
            pl.BlockSpec((HY_SHORT, c), lambda bi, j: (0, j)),
            pl.BlockSpec((1, c), lambda bi, j: (0, j)),
        ],
        out_specs=pl.BlockSpec((None, None, s // p, p * c), lambda bi, j: (j, bi, 0, 0)),
        out_shape=jax.ShapeDtypeStruct((3, b, s // p, p * c), F32),
        scratch_shapes=[pltpu.VMEM((c // LANES, s, LANES), F32)],
        compiler_params=_cparams("parallel", "parallel"),
        name="hy_short_conv",
    )(hy_in, conv_w, conv_b.reshape(1, c3))


def _filter_kernel(band_ref, w1_ref, b1_ref, f1_ref, w2_ref, b2_ref, f2_ref, w3_ref, delta_ref, o_ref,
                   *, seq):
    tr = o_ref.shape[1]
    c = delta_ref.shape[-1]
    n = pl.program_id(0) * tr + lax.broadcasted_iota(jnp.int32, (tr, LANES), 0)
    lane = lax.broadcasted_iota(jnp.int32, (tr, LANES), 1)
    pos = jnp.where(n < seq, n, 2 * seq - n).astype(F32)
    t = pos / (seq - 1)
    ang = (2.0 * math.pi * pos / seq) * band_ref[0:1, :] + band_ref[1:2, :]
    z = jnp.where(lane == 0, t, jnp.where(lane <= 2 * HY_BANDS, jnp.cos(ang), 0.0))
    h = jnp.sin(f1_ref[...] * (jnp.dot(z, w1_ref[...], precision=HIGHEST, preferred_element_type=F32)
                               + b1_ref[...]))
    h = jnp.sin(f2_ref[...] * (jnp.dot(h, w2_ref[...], precision=HIGHEST, preferred_element_type=F32)
                               + b2_ref[...]))
    h = jnp.dot(h, w3_ref[...], precision=HIGHEST, preferred_element_type=F32)
    t_c = jnp.concatenate([t] * (c // LANES), axis=-1)
    window = jnp.exp(-t_c * delta_ref[...]) + HY_DECAY_SHIFT
    live = jnp.concatenate([n] * (c // LANES), axis=-1) != seq
    for o in range(HY_ORDER):
        o_ref[o] = jnp.where(live, h[:, o * c:(o + 1) * c] * window, 0.0)


def _hyena_filters(seq, w1, b1, f1, w2, b2, f2, w3, *, tr=512):
    emb, hid = w1.shape
    c = w3.shape[1] // (HY_ORDER * 2)
    n = 2 * seq
    bands = np.zeros((2, LANES), np.float32)
    bands[1, 1 + HY_BANDS:1 + 2 * HY_BANDS] = 0.5 * np.pi
    bvals = np.linspace(1e-4, HY_BANDS - 1, HY_BANDS, dtype=np.float32)
    bands[0, 1:1 + HY_BANDS] = bvals
    bands[0, 1 + HY_BANDS:1 + 2 * HY_BANDS] = bvals
    max_decay = math.log(HY_DECAY_TARGET) / HY_FAST_DECAY_PCT
    min_decay = math.log(HY_DECAY_TARGET) / HY_SLOW_DECAY_PCT
    deltas = np.abs(np.linspace(min_decay, max_decay, c, dtype=np.float32)).reshape(1, c)
    w1p = jnp.pad(w1, ((0, LANES - emb), (0, 0)))
    w3d = w3.reshape(hid, HY_ORDER, 2, c).transpose(2, 0, 1, 3).reshape(2, hid, HY_ORDER * c)
    half_tiles = seq // tr
    kern = functools.partial(_filter_kernel, seq=seq)
    return pl.pallas_call(
        kern,
        grid=(n // tr,),
        in_specs=[
            _const_spec((2, LANES)),
            _const_spec((LANES, hid)), _const_spec((1, hid)), _const_spec((1, hid)),
            _const_spec((hid, hid)), _const_spec((1, hid)), _const_spec((1, hid)),
            pl.BlockSpec((None, hid, HY_ORDER * c), lambda i: (i // half_tiles, 0, 0)),
            _const_spec((1, c)),
        ],
        out_specs=pl.BlockSpec((HY_ORDER, tr, c), lambda i: (0, i, 0)),
        out_shape=jax.ShapeDtypeStruct((HY_ORDER, n, c), F32),
        compiler_params=_cparams("parallel"),
        name="hy_filter",
    )(jnp.asarray(bands), w1p, b1.reshape(1, hid), f1.reshape(1, hid), w2, b2.reshape(1, hid),
      f2.reshape(1, hid), w3d, jnp.asarray(deltas))


def _dft_tables(p):
    idx = np.arange(p)
    ang = 2.0 * np.pi * np.outer(idx, idx) / p
    fr, fi = np.cos(ang), -np.sin(ang)
    tang = 2.0 * np.pi * np.outer(idx, idx) / (p * p)
    twr, twi = np.cos(tang), -np.sin(tang)
    return fr, fi, twr, twi


def _split_bf16(x):
    hi = x.astype(BF16)
    return hi, (x - hi.astype(F32)).astype(BF16)


def _dft_dot(mat, data, *, split):
    dot = functools.partial(jnp.dot, preferred_element_type=F32)
    if not split:
        return dot(mat.astype(BF16), data.astype(BF16))
    mh, ml = _split_bf16(mat)
    dh, dl = _split_bf16(data)
    return dot(mh, dh) + (dot(mh, dl) + dot(ml, dh))


def _stage_a_kernel(x_ref, m_ref, o_ref, *, split):
    x = jnp.concatenate([x_ref[0], x_ref[1]], axis=0)
    o_ref[0] = _dft_dot(m_ref[...], x, split=split).astype(o_ref.dtype)


def _stage_a(x4, idx, mat, *, split, ct=2048):
    _, g2, hp, cols = x4.shape
    rows = mat.shape[0]
    return pl.pallas_call(
        functools.partial(_stage_a_kernel, split=split),
        grid=(g2 // 2, cols // ct),
        in_specs=[
            pl.BlockSpec((None, 2, hp, ct), lambda g, j: (idx, g, 0, j)),
            _const_spec(mat.shape),
        ],
        out_specs=pl.BlockSpec((1, rows, ct), lambda g, j: (g, 0, j)),
        out_shape=jax.ShapeDtypeStruct((g2 // 2, rows, cols), F32 if split else BF16),
        compiler_params=_cparams("parallel", "parallel"),
        name="hy_dft_a",
    )(x4, mat)


def _fwd_matrix(fr, fi, twr, twi):
    gr = fr * twr - fi * twi
    gi = fr * twi + fi * twr
    return jnp.concatenate([jnp.concatenate([gr, -gi], axis=1), jnp.concatenate([gi, gr], axis=1)], axis=0)


def _stack_pairs(a_ref, j):
    g = a_ref.shape[0]
    re = jnp.concatenate([a_ref[i, 0, j] for i in range(g)], axis=-1)
    im = jnp.concatenate([a_ref[i, 1, j] for i in range(g)], axis=-1)
    return jnp.concatenate([re, im], axis=0)


def _stage_c_kernel(a_ref, fr_ref, fi_ref, twr_ref, twi_ref, o_ref):
    g, _, kk, p, c = a_ref.shape
    fr, fi = fr_ref[...], fi_ref[...]
    for j in range(kk):
        gs = _fwd_matrix(fr, fi, twr_ref[j], twi_ref[j])
        x = _dft_dot(gs, _stack_pairs(a_ref, j), split=True)
        for i in range(g):
            o_ref[i, 0, j] = x[:p, i * c:(i + 1) * c]
            o_ref[i, 1, j] = x[p:, i * c:(i + 1) * c]


def _conv_core_kernel(a_ref, kh_ref, fr_ref, fi_ref, twr_ref, twi_ref, twbr_ref, twbi_ref, o_ref):
    g, _, kk, p, c = a_ref.shape
    fr, fi = fr_ref[...], fi_ref[...]
    xs = [_dft_dot(_fwd_matrix(fr, fi, twr_ref[j], twi_ref[j]), _stack_pairs(a_ref, j), split=False)
          for j in range(kk)]
    ys = []
    for j, x in enumerate(xs):
        xr, xi = x[:p], x[p:]
        kr = jnp.concatenate([kh_ref[0, j]] * g, axis=-1)
        ki = jnp.concatenate([kh_ref[1, j]] * g, axis=-1)
        ys.append(jnp.concatenate([xr * kr - xi * ki, xr * ki + xi * kr], axis=0))
    rs = []
    for j, y in enumerate(ys):
        br = fr * twbr_ref[j] - fi * twbi_ref[j]
        bi = -(fr * twbi_ref[j] + fi * twbr_ref[j])
        bs = jnp.concatenate([jnp.concatenate([br, -bi], axis=1), jnp.concatenate([bi, br], axis=1)], axis=0)
        rs.append(_dft_dot(bs, y, split=False).astype(o_ref.dtype))
    for j, r in enumerate(rs):
        for i in range(g):
            o_ref[i, 0, j] = r[:p, i * c:(i + 1) * c]
            o_ref[i, 1, j] = r[p:, i * c:(i + 1) * c]


def _tw_specs(p, kk):
    row = pl.BlockSpec((kk, 1, p), lambda i: (i, 0, 0))
    return row


def _stage_c(a5, tabs, *, kk=4):
    g, _, p, _, c = a5.shape
    fr, fi, twr, twi = tabs
    blk = pl.BlockSpec((g, 2, kk, p, c), lambda i: (0, 0, i, 0, 0))
    return pl.pallas_call(
        _stage_c_kernel,
        grid=(p // kk,),
        in_specs=[blk, _const_spec((p, p)), _const_spec((p, p)), _tw_specs(p, kk), _tw_specs(p, kk)],
        out_specs=blk,
        out_shape=jax.ShapeDtypeStruct(a5.shape, F32),
        compiler_params=_cparams("parallel"),
        name="hy_dft_c",
    )(a5, fr, fi, twr.reshape(p, 1, p), twi.reshape(p, 1, p))


def _conv_core(a5, khat, order, tabs, twb, *, kk=8):
    g, _, p, _, c = a5.shape
    fr, fi, twr, twi = tabs
    twbr, twbi = twb
    blk = pl.BlockSpec((g, 2, kk, p, c), lambda i: (0, 0, i, 0, 0))
    return pl.pallas_call(
        _conv_core_kernel,
        grid=(p // kk,),
        in_specs=[
            blk,
            pl.BlockSpec((None, 2, kk, p, c), lambda i: (order, 0, i, 0, 0)),
            _const_spec((p, p)), _const_spec((p, p)), _tw_specs(p, kk), _tw_specs(p, kk),
            pl.BlockSpec((kk, p, p), lambda i: (i, 0, 0)),
            pl.BlockSpec((kk, p, p), lambda i: (i, 0, 0)),
        ],
        out_specs=blk,
        out_shape=jax.ShapeDtypeStruct(a5.shape, BF16),
        compiler_params=_cparams("parallel"),
        name="hy_conv_core",
    )(a5, khat, fr, fi, twr.reshape(p, 1, p), twi.reshape(p, 1, p), twbr, twbi)


def _stage_b_kernel(r_ref, m_ref, u_ref, gate_ref, skip_ref, o_ref):
    y = _dft_dot(m_ref[...], r_ref[0], split=False)
    hp = u_ref.shape[1]
    skip = skip_ref[...]
    o_ref[0] = gate_ref[0] * (y[:hp] + skip * u_ref[0])
    o_ref[1] = gate_ref[1] * (y[hp:] + skip * u_ref[1])


def _stage_b(r3, mat, u4, u_idx, gate4, gate_idx, skip_row, *, ct=2048):
    g, rows, cols = r3.shape
    hp = mat.shape[0] // 2
    nb = u4.shape[1]
    return pl.pallas_call(
        _stage_b_kernel,
        grid=(g, cols // ct),
        in_specs=[
            pl.BlockSpec((1, rows, ct), lambda gi, j: (gi, 0, j)),
            _const_spec(mat.shape),
            pl.BlockSpec((None, 2, hp, ct), lambda gi, j: (u_idx, gi, 0, j)),
            pl.BlockSpec((None, 2, hp, ct), lambda gi, j: (gate_idx, gi, 0, j)),
            pl.BlockSpec((1, ct), lambda gi, j: (0, j)),
        ],
        out_specs=pl.BlockSpec((2, hp, ct), lambda gi, j: (gi, 0, j)),
        out_shape=jax.ShapeDtypeStruct((nb, hp, cols), F32),
        compiler_params=_cparams("parallel", "parallel"),
        name="hy_dft_b",
    )(r3, mat, u4, gate4, skip_row)


def _hyena(hy_in, conv_w, conv_b, w1, b1, f1, w2, b2, f2, w3, skip):
    b, s, c3 = hy_in.shape
    c = c3 // 3
    p = math.isqrt(2 * s)
    assert p * p == 2 * s and b % 2 == 0
    hp = p // 2
    cols = p * c
    fr, fi, twr, twi = _dft_tables(p)
    scale = 1.0 / (p * p)
    mat_a = np.block([[fr[:, :hp], -fi[:, :hp]], [fi[:, :hp], fr[:, :hp]]]).astype(np.float32)
    mat_k = np.concatenate([fr, fi], axis=0).astype(np.float32)
    br, bi = fr[:hp] * scale, -fi[:hp] * scale
    mat_b = np.block([[br, -bi], [bi, br]]).astype(np.float32)
    tabs = tuple(jnp.asarray(t.astype(np.float32)) for t in (fr, fi, twr, twi))
    twb = tuple(jnp.broadcast_to(t[:, :, None], (p, p, p)) for t in tabs[2:])

    uvx = _short_conv(hy_in, conv_w, conv_b, p)
    kern = _hyena_filters(s, w1, b1, f1, w2, b2, f2, w3)
    ka = _stage_a(kern.reshape(1, HY_ORDER * 2, hp, cols), 0, jnp.asarray(mat_k), split=True)
    khat = _stage_c(ka.reshape(HY_ORDER, 2, p, p, c), tabs)

    skip_rows = jnp.tile(skip, (1, p))
    g = b // 2

    def long_conv(u4, u_idx, gate_idx, order):
        a = _stage_a(u4, u_idx, jnp.asarray(mat_a), split=False)
        r = _conv_core(a.reshape(g, 2, p, p, c), khat, order, tabs, twb)
        return _stage_b(r.reshape(g, 2 * p, cols), jnp.asarray(mat_b), u4, u_idx, uvx, gate_idx,
                        skip_rows[order:order + 1])

    z = long_conv(uvx, 0, 1, 0)
    y = long_conv(z[None], 0, 2, 1)
    return y.reshape(b, s, c)


def kernel(x, ffn1_norm, ffn1_w_gate, ffn1_w_up, ffn1_w_down, mix_norm, w_in, hy_conv_w, hy_conv_b, hy_filt_w1, hy_filt_b1, hy_filt_freq1, hy_filt_w2, hy_filt_b2, hy_filt_freq2, hy_filt_w3, hy_skip, hy_out_norm, mla_q_norm, mla_w_uq, mla_kv_norm, mla_w_ukv, mla_out_norm, gla_w_gate_fwd, gla_b_gate_fwd, gla_w_gate_bwd, gla_b_gate_bwd, gla_head_norm, w_out, ffn2_norm, ffn2_w_gate, ffn2_w_up, ffn2_w_down, final_norm):
    b, s, d = x.shape
    depth = ffn1_norm.shape[0]
    t = b * s
    for l in range(depth):
        x2 = _ffn(x.reshape(t, d), ffn1_norm[l], ffn1_w_gate[l], ffn1_w_up[l], ffn1_w_down[l], final_norm,
                  final=False)
        (hy_in, q, k, v, gq, gk, gv, gf, gb, gr) = _in_proj(
            x2.reshape(b, s, d), mix_norm[l], w_in[l], mla_q_norm[l], mla_w_uq[l], mla_kv_norm[l],
            mla_w_ukv[l], gla_w_gate_fwd[l], gla_b_gate_fwd[l], gla_w_gate_bwd[l], gla_b_gate_bwd[l])
        y_hy = _hyena(hy_in, hy_conv_w[l], hy_conv_b[l], hy_filt_w1[l], hy_filt_b1[l], hy_filt_freq1[l],
                      hy_filt_w2[l], hy_filt_b2[l], hy_filt_freq2[l], hy_filt_w3[l], hy_skip[l])
        o_mla = _mla_attention(q, k, v)
        y_gla = _gla(gq, gk, gv, gf, gb, gr, gla_head_norm[l])
        mix = (y_hy.reshape(t, -1), o_mla.reshape(t, -1), y_gla.reshape(t, -1), hy_out_norm[l],
               mla_out_norm[l], w_out[l])
        x = _ffn(x2, ffn2_norm[l], ffn2_w_gate[l], ffn2_w_up[l], ffn2_w_down[l], final_norm,
                 final=(l == depth - 1), mix=mix).reshape(b, s, d)
    return x
```

```python
import functools
import math

import jax
import jax.numpy as jnp
import numpy as np
from jax import lax
from jax.experimental import pallas as pl
from jax.experimental.pallas import tpu as pltpu

F32 = jnp.float32
BF16 = jnp.bfloat16
HIGHEST = lax.Precision.HIGHEST

NORM_EPS = 1e-6
HY_ORDER = 2
HY_SHORT = 3
HY_BANDS = 16
HY_FAST_DECAY_PCT = 0.3
HY_SLOW_DECAY_PCT = 1.5
HY_DECAY_TARGET = 1e-2
HY_DECAY_SHIFT = 0.0
MLA_HEADS = 4
MLA_QK_NOPE = 128
MLA_QK_ROPE = 64
MLA_V_HEAD = 128
ROPE_BASE = 10000.0
GLA_HEADS = 4
GLA_GATE_RANK = 16
GLA_GATE_NORM = 16.0

LANES = 128
GLA_CHUNK = 64
GLA_SUB = 16
GLA_FACTORED_MIN_GATE = -1.0
VMEM_LIMIT = 56 * 1024 * 1024

NEG_BIG = -1e30


def _cparams(*sem):
    return pltpu.CompilerParams(dimension_semantics=sem, vmem_limit_bytes=VMEM_LIMIT)


def _rms(x, g):
    ms = jnp.mean(x * x, axis=-1, keepdims=True)
    return x * lax.rsqrt(ms + NORM_EPS) * g


def _const_spec(shape):
    nd = len(shape)
    return pl.BlockSpec(shape, lambda *_: (0,) * nd)


def _ffn_kernel(x_ref, g_ref, wg_ref, wu_ref, wd_ref, fg_ref, o_ref, *, ff_chunk, final):
    _ffn_body(x_ref[...], g_ref, wg_ref, wu_ref, wd_ref, fg_ref, o_ref, ff_chunk=ff_chunk, final=final)


def _mix_ffn_kernel(x_ref, hy_ref, mla_ref, gla_ref, ghy_ref, gmla_ref, wout_ref,
                    g_ref, wg_ref, wu_ref, wd_ref, fg_ref, o_ref, *, ff_chunk, final):
    y_hy = _rms(hy_ref[...], ghy_ref[...]).astype(BF16)
    y_mla = _rms(mla_ref[...], gmla_ref[...]).astype(BF16)
    cat = jnp.concatenate([y_hy, y_mla, gla_ref[...]], axis=-1)
    x = x_ref[...] + jnp.dot(cat, wout_ref[...], preferred_element_type=F32)
    _ffn_body(x, g_ref, wg_ref, wu_ref, wd_ref, fg_ref, o_ref, ff_chunk=ff_chunk, final=final)


def _ffn_body(x, g_ref, wg_ref, wu_ref, wd_ref, fg_ref, o_ref, *, ff_chunk, final):
    xn = _rms(x, g_ref[...]).astype(BF16)
    d_ff = wg_ref.shape[1]
    acc = jnp.zeros_like(x)
    for c in range(0, d_ff, ff_chunk):
        gate = jnp.dot(xn, wg_ref[:, c:c + ff_chunk], preferred_element_type=F32)
        up = jnp.dot(xn, wu_ref[:, c:c + ff_chunk], preferred_element_type=F32)
        h = (gate * jax.nn.sigmoid(gate) * up).astype(BF16)
        acc = acc + jnp.dot(h, wd_ref[c:c + ff_chunk, :], preferred_element_type=F32)
    y = x + 0.5 * acc
    if final:
        y = _rms(y, fg_ref[...])
    o_ref[...] = y


def _resident_spec(shape):
    nd = len(shape)
    return pl.BlockSpec(shape, lambda *_: (0,) * nd, pipeline_mode=pl.Buffered(1))


def _ffn(x2d, norm_g, w_gate, w_up, w_down, final_g, *, final, mix=None, tm=512, ff_chunk=256):
    t, d = x2d.shape
    d_ff = w_gate.shape[1]
    tok = lambda w: pl.BlockSpec((tm, w), lambda i: (i, 0))
    ffn_specs = [_const_spec((1, d)), _resident_spec((d, d_ff)), _resident_spec((d, d_ff)),
                 _resident_spec((d_ff, d)), _const_spec((1, d))]
    ffn_args = (norm_g.reshape(1, d), w_gate.astype(BF16), w_up.astype(BF16), w_down.astype(BF16),
                final_g.reshape(1, d))
    if mix is None:
        kern, name = _ffn_kernel, "ffn"
        in_specs = [tok(d)] + ffn_specs
        args = (x2d,) + ffn_args
    else:
        y_hy, o_mla, y_gla, g_hy, g_mla, w_out = mix
        kern, name = _mix_ffn_kernel, "mix_ffn"
        in_specs = [tok(d), tok(y_hy.shape[1]), tok(o_mla.shape[1]), tok(y_gla.shape[1]),
                    _const_spec((1, y_hy.shape[1])), _const_spec((1, o_mla.shape[1])),
                    _resident_spec(w_out.shape)] + ffn_specs
        args = (x2d, y_hy, o_mla, y_gla, g_hy.reshape(1, -1), g_mla.reshape(1, -1),
                w_out.astype(BF16)) + ffn_args
    return pl.pallas_call(
        functools.partial(kern, ff_chunk=ff_chunk, final=final),
        grid=(t // tm,),
        in_specs=in_specs,
        out_specs=tok(d),
        out_shape=jax.ShapeDtypeStruct((t, d), F32),
        compiler_params=_cparams("parallel"),
        name=name,
    )(*args)


def _log_sigmoid(z):
    return jnp.minimum(z, 0.0) - jnp.log(1.0 + jnp.exp(-jnp.abs(z)))


def _in_proj_kernel(x_ref, g_ref, win_ref, qn_ref, wuq_ref, kvn_ref, wukv_ref, cos_ref, sin_ref,
                    wgate_ref, bgate_ref,
                    hy_ref, q_ref, k_ref, v_ref, gq_ref, gk_ref, gv_ref, gf_ref, gb_ref, gr_ref,
                    *, off, q_scale, gla_q_scale):
    x = x_ref[0]
    xn = _rms(x, g_ref[...]).astype(BF16)
    proj = jnp.dot(xn, win_ref[...], preferred_element_type=F32)

    def blk(name):
        lo, hi = off[name]
        return proj[:, lo:hi]

    hy_ref[0] = blk("hy")

    qn = _rms(blk("cq"), qn_ref[...]).astype(BF16)
    qf = jnp.dot(qn, wuq_ref[...], preferred_element_type=F32)
    kvn = _rms(blk("ckv"), kvn_ref[...]).astype(BF16)
    kvf = jnp.dot(kvn, wukv_ref[...], preferred_element_type=F32)
    cos = cos_ref[...]
    sin = sin_ref[...]
    k_rope = blk("kr") * cos + blk("kr_rot") * sin
    ones = jnp.ones_like(k_rope)
    qw = 3 * LANES
    kvw = 2 * LANES
    for h in range(MLA_HEADS):
        q_nope = qf[:, h * qw:h * qw + LANES]
        q_rope = (qf[:, h * qw + LANES:h * qw + 2 * LANES] * cos
                  + qf[:, h * qw + 2 * LANES:(h + 1) * qw] * sin)
        q_ref[0, h] = (jnp.concatenate([q_nope, q_rope], axis=-1) * q_scale).astype(BF16)
        k_ref[0, h] = jnp.concatenate([kvf[:, h * kvw:h * kvw + LANES], k_rope], axis=-1).astype(BF16)
        v_ref[0, h] = jnp.concatenate([kvf[:, h * kvw + LANES:(h + 1) * kvw], ones], axis=-1).astype(BF16)

    gq_ref[0] = blk("gq") * gla_q_scale
    gk_ref[0] = blk("gk")
    gv_ref[0] = blk("gv")
    z = jnp.dot(blk("gate"), wgate_ref[...], precision=HIGHEST, preferred_element_type=F32) + bgate_ref[...]
    gates = _log_sigmoid(z) / GLA_GATE_NORM
    gk_w = gf_ref.shape[-1]
    gf_ref[0] = gates[:, :gk_w]
    gb_ref[0] = gates[:, gk_w:]
    gr_ref[0] = blk("gr")


def _rope_rot_cols(w):
    half = w.shape[-1] // 2
    return jnp.concatenate([-w[..., half:], w[..., :half]], axis=-1)


def _pad_cols(w, width):
    return jnp.pad(w, ((0, 0), (0, width - w.shape[-1])))


def _in_proj(x, mix_g, w_in, q_norm, w_uq, kv_norm, w_ukv, w_gf, b_gf, w_gb, b_gb, *, tm=512):
    b, s, d = x.shape
    hy_in = 3 * (d // 4)
    q_lora = d // 4
    kv_lora = d // 8
    gla_w = d // 4
    gla_key = gla_w // 2
    rope = MLA_QK_ROPE
    splits = (hy_in, q_lora, kv_lora, rope, gla_key, gla_key, gla_w, GLA_GATE_RANK, GLA_GATE_RANK, gla_w)
    offs = np.concatenate([[0], np.cumsum(splits)])
    (w_hy, w_cq, w_ckv, w_kr, w_gq, w_gk, w_gv, w_gfl, w_gbl, w_gr) = [
        w_in[:, offs[i]:offs[i + 1]] for i in range(len(splits))]
    pieces = [
        ("hy", w_hy), ("cq", w_cq), ("ckv", w_ckv),
        ("kr", _pad_cols(w_kr, LANES)), ("kr_rot", _pad_cols(_rope_rot_cols(w_kr), LANES)),
        ("gq", w_gq), ("gk", w_gk), ("gv", w_gv),
        ("gate", _pad_cols(jnp.concatenate([w_gfl, w_gbl], axis=1), LANES)),
        ("gr", w_gr),
    ]
    off = {}
    pos = 0
    for name, w in pieces:
        off[name] = (pos, pos + w.shape[1])
        pos += w.shape[1]
    win_r = jnp.concatenate([w for _, w in pieces], axis=1).astype(BF16)
    n_in = win_r.shape[1]

    dqk = MLA_QK_NOPE + MLA_QK_ROPE
    wq = w_uq.reshape(q_lora, MLA_HEADS, dqk)
    wq_nope = wq[..., :MLA_QK_NOPE]
    wq_rope = wq[..., MLA_QK_NOPE:]
    zpad = jnp.zeros((q_lora, MLA_HEADS, LANES - rope), F32)
    wuq_r = jnp.concatenate([wq_nope, wq_rope, zpad, _rope_rot_cols(wq_rope), zpad], axis=-1)
    wuq_r = wuq_r.reshape(q_lora, MLA_HEADS * 3 * LANES).astype(BF16)
    wukv_r = w_ukv.astype(BF16)

    half = rope // 2
    inv = ROPE_BASE ** (-jnp.arange(half, dtype=F32) * 2.0 / rope)
    ang = jnp.arange(s, dtype=F32)[:, None] * inv[None, :]
    zt = jnp.zeros((s, LANES - rope), F32)
    cos_t = jnp.concatenate([jnp.cos(ang), jnp.cos(ang), zt], axis=1)
    sin_t = jnp.concatenate([jnp.sin(ang), jnp.sin(ang), zt], axis=1)

    wgate = jnp.zeros((LANES, 2 * gla_key), F32)
    wgate = wgate.at[:GLA_GATE_RANK, :gla_key].set(w_gf)
    wgate = wgate.at[GLA_GATE_RANK:2 * GLA_GATE_RANK, gla_key:].set(w_gb)
    bgate = jnp.concatenate([b_gf, b_gb]).reshape(1, 2 * gla_key)

    kern = functools.partial(_in_proj_kernel, off=off, q_scale=float(dqk) ** -0.5 * math.log2(math.e),
                             gla_q_scale=float(gla_key // GLA_HEADS) ** -0.5)
    tok = lambda w: pl.BlockSpec((1, tm, w), lambda bi, i: (bi, i, 0))
    head = lambda w: pl.BlockSpec((1, MLA_HEADS, tm, w), lambda bi, i: (bi, 0, i, 0))
    outs = pl.pallas_call(
        kern,
        grid=(b, s // tm),
        in_specs=[
            tok(d),
            _const_spec((1, d)),
            _const_spec((d, n_in)),
            _const_spec((1, q_lora)),
            _const_spec((q_lora, MLA_HEADS * 3 * LANES)),
            _const_spec((1, kv_lora)),
            _const_spec((kv_lora, MLA_HEADS * 2 * LANES)),
            pl.BlockSpec((tm, LANES), lambda bi, i: (i, 0)),
            pl.BlockSpec((tm, LANES), lambda bi, i: (i, 0)),
            _const_spec((LANES, 2 * gla_key)),
            _const_spec((1, 2 * gla_key)),
        ],
        out_specs=[
            tok(hy_in),
            head(2 * LANES), head(2 * LANES), head(2 * LANES),
            tok(gla_key), tok(gla_key), tok(gla_w), tok(gla_key), tok(gla_key), tok(gla_w),
        ],
        out_shape=[
            jax.ShapeDtypeStruct((b, s, hy_in), F32),
            jax.ShapeDtypeStruct((b, MLA_HEADS, s, 2 * LANES), BF16),
            jax.ShapeDtypeStruct((b, MLA_HEADS, s, 2 * LANES), BF16),
            jax.ShapeDtypeStruct((b, MLA_HEADS, s, 2 * LANES), BF16),
            jax.ShapeDtypeStruct((b, s, gla_key), F32),
            jax.ShapeDtypeStruct((b, s, gla_key), F32),
            jax.ShapeDtypeStruct((b, s, gla_w), F32),
            jax.ShapeDtypeStruct((b, s, gla_key), F32),
            jax.ShapeDtypeStruct((b, s, gla_key), F32),
            jax.ShapeDtypeStruct((b, s, gla_w), F32),
        ],
        compiler_params=_cparams("parallel", "parallel"),
        name="in_proj",
    )(x, mix_g.reshape(1, d), win_r, q_norm.reshape(1, -1), wuq_r, kv_norm.reshape(1, -1), wukv_r,
      cos_t, sin_t, wgate, bgate)
    return outs


def _attn_kernel(q_ref, k_ref, v_ref, o_ref, s0_ref, s1_ref, p_ref, acc_ref, m_ref, alpha_ref, *, tk, rb):
    q = q_ref[...]
    tq = q.shape[0]
    nk = k_ref.shape[0] // tk
    dv = o_ref.shape[-1]

    def scores(j, dst_ref):
        start = pl.multiple_of(j * tk, tk)
        dst_ref[...] = lax.dot_general(q, k_ref[pl.ds(start, tk), :], (((1,), (1,)), ((), ())),
                                       preferred_element_type=F32)

    def consume(j, cur_ref, nxt_ref):
        if nxt_ref is not None:
            scores(j + 1, nxt_ref)
        for r in range(tq // rb):
            rows = slice(r * rb, (r + 1) * rb)
            tiles = [cur_ref[rows, t * LANES:(t + 1) * LANES] for t in range(tk // LANES)]
            mx = functools.reduce(jnp.maximum, tiles)
            m_old = m_ref[rows, :]
            m_new = jnp.maximum(m_old, jnp.broadcast_to(jnp.max(mx, axis=-1, keepdims=True), m_old.shape))
            alpha_ref[rows, :] = jnp.exp2(m_old - m_new)
            for t, s_t in enumerate(tiles):
                p_ref[rows, t * LANES:(t + 1) * LANES] = jnp.exp2(s_t - m_new).astype(BF16)
            m_ref[rows, :] = m_new
        start = pl.multiple_of(j * tk, tk)
        pv = jnp.dot(p_ref[...], v_ref[pl.ds(start, tk), :], preferred_element_type=F32)
        alpha = alpha_ref[...]
        for t in range(acc_ref.shape[1] // LANES):
            cols = slice(t * LANES, (t + 1) * LANES)
            acc_ref[:, cols] = acc_ref[:, cols] * alpha + pv[:, cols]

    m_ref[...] = jnp.full(m_ref.shape, -jnp.inf, F32)
    acc_ref[...] = jnp.zeros(acc_ref.shape, F32)
    scores(0, s0_ref)

    def pair(i, carry):
        consume(2 * i, s0_ref, s1_ref)
        consume(2 * i + 1, s1_ref, s0_ref)
        return carry

    lax.fori_loop(0, nk // 2 - 1, pair, 0)
    consume(nk - 2, s0_ref, s1_ref)
    consume(nk - 1, s1_ref, None)
    acc = acc_ref[...]
    o_ref[...] = acc[:, :dv] / acc[:, dv:]


def _mla_attention(q, k, v, *, tq=1024, tk=2048, rb=64):
    b, h, s, dq = q.shape
    dv = MLA_V_HEAD
    tk = min(tk, s // 2)
    assert (s // tk) % 2 == 0
    kern = functools.partial(_attn_kernel, tk=tk, rb=rb)
    return pl.pallas_call(
        kern,
        grid=(b, h, s // tq),
        in_specs=[
            pl.BlockSpec((None, None, tq, dq), lambda bi, hi, i: (bi, hi, i, 0)),
            pl.BlockSpec((None, None, s, dq), lambda bi, hi, i: (bi, hi, 0, 0)),
            pl.BlockSpec((None, None, s, 2 * dv), lambda bi, hi, i: (bi, hi, 0, 0)),
        ],
        out_specs=pl.BlockSpec((None, tq, dv), lambda bi, hi, i: (bi, i, hi)),
        out_shape=jax.ShapeDtypeStruct((b, s, h * dv), F32),
        scratch_shapes=[
            pltpu.VMEM((tq, tk), F32), pltpu.VMEM((tq, tk), F32), pltpu.VMEM((tq, tk), BF16),
            pltpu.VMEM((tq, 2 * dv), F32), pltpu.VMEM((tq, LANES), F32), pltpu.VMEM((tq, LANES), F32),
        ],
        compiler_params=_cparams("parallel", "parallel", "parallel"),
        name="mla_attn",
    )(q, k, v)


def _split3(x):
    x0 = x.astype(BF16)
    r1 = x - x0.astype(F32)
    x1 = r1.astype(BF16)
    x2 = (r1 - x1.astype(F32)).astype(BF16)
    return x0, x1, x2


def _chunk_cumsum(tri, g):
    g0, g1, g2 = _split3(g)
    dot = functools.partial(jnp.dot, preferred_element_type=F32)
    return dot(tri, g0) + (dot(tri, g1) + dot(tri, g2))


def _gla_intra(q, k, v, b2f, b2b, e3, hmask, vmask):
    c = q.shape[0]
    jj = lax.broadcasted_iota(jnp.int32, (GLA_SUB, GLA_SUB, LANES), 0)
    ii = lax.broadcasted_iota(jnp.int32, (GLA_SUB, GLA_SUB, LANES), 1)
    lower = jj <= ii
    row = lax.broadcasted_iota(jnp.int32, (c, LANES), 0)
    v16 = v.astype(BF16)
    outs = []
    for sb in range(c // GLA_SUB):
        r0, r1 = sb * GLA_SUB, (sb + 1) * GLA_SUB
        qs, ks, vs, bf, bb = q[r0:r1], k[r0:r1], v[r0:r1], b2f[r0:r1], b2b[r0:r1]
        d = jnp.where(lower, bf[None, :, :] - bf[:, None, :], bb[None, :, :] - bb[:, None, :])
        p = (qs[None, :, :] * ks[:, None, :]) * jnp.exp2(d)
        w = jnp.dot(p.reshape(GLA_SUB * GLA_SUB, LANES).astype(BF16), e3, preferred_element_type=F32)
        o_sb = jnp.sum(w.reshape(GLA_SUB, GLA_SUB, w.shape[-1]) * vs[:, None, :], axis=0)
        q_parts, k_parts = [], []
        if r0 > 0:
            ref = b2f[r0 - 1:r0]
            q_parts.append(qs * jnp.exp2(bf - ref))
            k_parts.append(k * jnp.exp2(jnp.where(row < r0, ref - b2f, NEG_BIG)))
        if r1 < c:
            ref = b2b[r1:r1 + 1]
            q_parts.append(qs * jnp.exp2(bb - ref))
            k_parts.append(k * jnp.exp2(jnp.where(row >= r1, ref - b2b, NEG_BIG)))
        qt = jnp.concatenate(q_parts, axis=-1)
        kt = jnp.concatenate(k_parts, axis=-1).astype(BF16)
        hm = jnp.concatenate([hmask] * len(q_parts), axis=-1)
        q_stack = jnp.concatenate([qt * hm[hh:hh + 1] for hh in range(GLA_HEADS)], axis=0).astype(BF16)
        a = lax.dot_general(q_stack, kt, (((1,), (1,)), ((), ())), preferred_element_type=F32)
        o_all = jnp.dot(a.astype(BF16), v16, preferred_element_type=F32)
        for hh in range(GLA_HEADS):
            o_sb = o_sb + o_all[hh * GLA_SUB:(hh + 1) * GLA_SUB] * vmask[hh:hh + 1]
        outs.append(o_sb)
    return jnp.concatenate(outs, axis=0)


def _gla_state_kernel(kf_ref, vf_ref, gf_ref, kb_ref, vb_ref, gb_ref, tril_ref, triu_ref, stmask_ref,
                      sf_ref, sb_ref, stf_ref, stb_ref):
    @pl.when(pl.program_id(1) == 0)
    def _():
        stf_ref[...] = jnp.zeros_like(stf_ref)
        stb_ref[...] = jnp.zeros_like(stb_ref)

    nc = kf_ref.shape[0] // GLA_CHUNK
    stmask = stmask_ref[...]
    chunks = [slice(c * GLA_CHUNK, (c + 1) * GLA_CHUNK) for c in range(nc)]
    tn = (((0,), (0,)), ((), ()))

    for reverse in (False, True):
        if reverse:
            k_ref, v_ref, g_ref, s_ref, st_ref, tri = kb_ref, vb_ref, gb_ref, sb_ref, stb_ref, triu_ref[...]
        else:
            k_ref, v_ref, g_ref, s_ref, st_ref, tri = kf_ref, vf_ref, gf_ref, sf_ref, stf_ref, tril_ref[...]
        g, k, v16 = g_ref[...], k_ref[...], v_ref[...].astype(BF16)
        beta = [_chunk_cumsum(tri, g[r]) for r in chunks]
        tot = [b[0:1] if reverse else b[GLA_CHUNK - 1:GLA_CHUNK] for b in beta]
        kd = [(k[r] * jnp.exp(t - b)).astype(BF16) for r, t, b in zip(chunks, tot, beta)]
        ut = [lax.dot_general(v16[r], x, tn, preferred_element_type=F32) for r, x in zip(chunks, kd)]
        dec = [jnp.exp(t) for t in tot]
        st = st_ref[...]
        for c in (reversed(range(nc)) if reverse else range(nc)):
            s_ref[c] = st.astype(BF16)
            st = st * dec[c] + ut[c] * stmask
        st_ref[...] = st


def _gla_intra_factored(qf, qb, k, v16, b2f, b2b, hmask, vmask, lower):
    nt = (((1,), (1,)), ((), ()))
    n = len(k)
    c = k[0].shape[0]

    def head_rows(x):
        return jnp.concatenate([x * hmask[hh:hh + 1] for hh in range(GLA_HEADS)], axis=0).astype(BF16)

    kf = [(k[i] * jnp.exp2(-b2f[i])).astype(BF16) for i in range(n)]
    kb = [(k[i] * jnp.exp2(-b2b[i])).astype(BF16) for i in range(n)]
    hf = [head_rows(x) for x in qf]
    hb = [head_rows(x) for x in qb]
    a_f = [lax.dot_general(hf[i], kf[i], nt, preferred_element_type=F32) for i in range(n)]
    a_b = [lax.dot_general(hb[i], kb[i], nt, preferred_element_type=F32) for i in range(n)]
    a = [jnp.where(lower, a_f[i], a_b[i]).astype(BF16) for i in range(n)]
    o_all = [jnp.dot(a[i], v16[i], preferred_element_type=F32) for i in range(n)]
    outs = []
    for i in range(n):
        o = o_all[i][0:c] * vmask[0:1]
        for hh in range(1, GLA_HEADS):
            o = o + o_all[i][hh * c:(hh + 1) * c] * vmask[hh:hh + 1]
        outs.append(o)
    return outs


def _gla_out_kernel(q_ref, k_ref, v_ref, gf_ref, gb_ref, r_ref, sf_ref, sb_ref,
                    tril_ref, triu_ref, e3_ref, hmask_ref, vmask_ref, seg_ref, gn_ref, o_ref):
    nc = q_ref.shape[0] // GLA_CHUNK
    tril, triu = tril_ref[...], triu_ref[...]
    e3, hmask, vmask = e3_ref[...], hmask_ref[...], vmask_ref[...]
    seg, gn = seg_ref[...], gn_ref[...]
    log2e = math.log2(math.e)
    ri = lax.broadcasted_iota(jnp.int32, (GLA_HEADS * GLA_CHUNK, GLA_CHUNK), 0) % GLA_CHUNK
    cj = lax.broadcasted_iota(jnp.int32, (GLA_HEADS * GLA_CHUNK, GLA_CHUNK), 1)
    lower = cj <= ri

    dot = functools.partial(jnp.dot, preferred_element_type=F32)
    nt = (((1,), (1,)), ((), ()))

    def finish(o, r):
        s0, s1, s2 = _split3(o * o)
        ms = dot(s0, seg) + (dot(s1, seg) + dot(s2, seg))
        return (o * lax.rsqrt(ms + NORM_EPS) * gn * (r * jax.nn.sigmoid(r))).astype(o_ref.dtype)

    def elementwise_step(c, carry):
        rows = pl.ds(pl.multiple_of(c * GLA_CHUNK, GLA_CHUNK), GLA_CHUNK)
        q, k, v = q_ref[rows, :], k_ref[rows, :], v_ref[rows, :]
        b2f = _chunk_cumsum(tril, gf_ref[rows, :] * log2e)
        b2b = _chunk_cumsum(triu, gb_ref[rows, :] * log2e)
        qd = jnp.concatenate([q * jnp.exp2(b2f), q * jnp.exp2(b2b)], axis=-1).astype(BF16)
        st = jnp.concatenate([sf_ref[c], sb_ref[c]], axis=-1)
        o = lax.dot_general(qd, st, nt, preferred_element_type=F32)
        o = o + _gla_intra(q, k, v, b2f, b2b, e3, hmask, vmask)
        o_ref[rows, :] = finish(o, r_ref[rows, :])
        return carry

    def factored_block():
        chunks = [slice(c * GLA_CHUNK, (c + 1) * GLA_CHUNK) for c in range(nc)]
        g2f, g2b = gf_ref[...] * log2e, gb_ref[...] * log2e
        q, k, v16 = q_ref[...], k_ref[...], v_ref[...].astype(BF16)
        b2f = [_chunk_cumsum(tril, g2f[r]) for r in chunks]
        b2b = [_chunk_cumsum(triu, g2b[r]) for r in chunks]
        qf = [q[r] * jnp.exp2(b) for r, b in zip(chunks, b2f)]
        qb = [q[r] * jnp.exp2(b) for r, b in zip(chunks, b2b)]
        inter = [lax.dot_general(jnp.concatenate([qf[c], qb[c]], axis=-1).astype(BF16),
                                 jnp.concatenate([sf_ref[c], sb_ref[c]], axis=-1), nt,
                                 preferred_element_type=F32) for c in range(nc)]
        intra = _gla_intra_factored(qf, qb, [k[r] for r in chunks], [v16[r] for r in chunks], b2f, b2b,
                                    hmask, vmask, lower)
        o = jnp.concatenate([a + b for a, b in zip(inter, intra)], axis=0)
        o_ref[...] = finish(o, r_ref[...])

    gmin = jnp.minimum(jnp.min(gf_ref[...]), jnp.min(gb_ref[...]))
    mild = gmin >= GLA_FACTORED_MIN_GATE

    @pl.when(mild)
    def _():
        factored_block()

    @pl.when(jnp.logical_not(mild))
    def _():
        lax.fori_loop(0, nc, elementwise_step, 0, unroll=4)


def _gla(gq, gk, gv, gf, gb, gr, head_norm, *, tb=512):
    b, s, kw = gq.shape
    vw = gv.shape[-1]
    nb = s // tb
    ncb = tb // GLA_CHUNK
    dk = kw // GLA_HEADS
    dv = vw // GLA_HEADS
    idx = np.arange(GLA_CHUNK)
    tril = jnp.asarray((idx[None, :] <= idx[:, None]).astype(np.float32)).astype(BF16)
    triu = jnp.asarray((idx[None, :] >= idx[:, None]).astype(np.float32)).astype(BF16)
    d_head = np.arange(kw) // dk
    e_head = np.arange(vw) // dv
    e3 = jnp.asarray((d_head[:, None] == e_head[None, :]).astype(np.float32)).astype(BF16)
    hmask = jnp.asarray((np.arange(GLA_HEADS)[:, None] == d_head[None, :]).astype(np.float32))
    vmask = jnp.asarray((np.arange(GLA_HEADS)[:, None] == e_head[None, :]).astype(np.float32))
    stmask = jnp.asarray((e_head[:, None] == d_head[None, :]).astype(np.float32))
    seg = jnp.asarray((e_head[:, None] == e_head[None, :]).astype(np.float32) / dv).astype(BF16)
    tri_spec = _const_spec((GLA_CHUNK, GLA_CHUNK))

    fwd = lambda w: pl.BlockSpec((None, tb, w), lambda bi, i: (bi, i, 0))
    bwd = lambda w: pl.BlockSpec((None, tb, w), lambda bi, i: (bi, nb - 1 - i, 0))
    st_shape = jax.ShapeDtypeStruct((b, s // GLA_CHUNK, vw, kw), BF16)
    s_f, s_b = pl.pallas_call(
        _gla_state_kernel,
        grid=(b, nb),
        in_specs=[fwd(kw), fwd(vw), fwd(kw), bwd(kw), bwd(vw), bwd(kw), tri_spec, tri_spec,
                  _const_spec((vw, kw))],
        out_specs=[pl.BlockSpec((None, ncb, vw, kw), lambda bi, i: (bi, i, 0, 0)),
                   pl.BlockSpec((None, ncb, vw, kw), lambda bi, i: (bi, nb - 1 - i, 0, 0))],
        out_shape=[st_shape, st_shape],
        scratch_shapes=[pltpu.VMEM((vw, kw), F32), pltpu.VMEM((vw, kw), F32)],
        compiler_params=_cparams("parallel", "arbitrary"),
        name="gla_state",
    )(gk, gv, gf, gk, gv, gb, tril, triu, stmask)

    st_spec = pl.BlockSpec((None, ncb, vw, kw), lambda bi, i: (bi, i, 0, 0))
    return pl.pallas_call(
        _gla_out_kernel,
        grid=(b, nb),
        in_specs=[fwd(kw), fwd(kw), fwd(vw), fwd(kw), fwd(kw), fwd(vw), st_spec, st_spec,
                  tri_spec, tri_spec, _const_spec((kw, vw)), _const_spec((GLA_HEADS, kw)),
                  _const_spec((GLA_HEADS, vw)), _const_spec((vw, vw)), _const_spec((1, vw))],
        out_specs=fwd(vw),
        out_shape=jax.ShapeDtypeStruct((b, s, vw), BF16),
        compiler_params=_cparams("parallel", "parallel"),
        name="gla_out",
    )(gq, gk, gv, gf, gb, gr, s_f, s_b, tril, triu, e3, hmask, vmask, seg,
      jnp.tile(head_norm, GLA_HEADS).reshape(1, vw))


def _short_conv_kernel(*refs, p, pitch):
    nt = len(refs) - 4
    x_refs, (w_ref, b_ref, o_ref, s_ref) = refs[:nt], refs[nt:]
    n = x_refs[0].shape[0]
    hp = n // p
    c = nt * LANES
    rowi = lax.broadcasted_iota(jnp.int32, (hp, LANES), 0)
    for t, x_ref in enumerate(x_refs):
        for n2 in range(hp):
            s_ref[t, n2 * pitch:n2 * pitch + p, :] = x_ref[n2 * p:(n2 + 1) * p, :]
    for t in range(nt):
        lanes = slice(t * LANES, (t + 1) * LANES)
        w0, w1, w2, bias = w_ref[0:1, lanes], w_ref[1:2, lanes], w_ref[2:3, lanes], b_ref[:, lanes]
        def slab(n1):
            return s_ref[t, pl.ds(n1, hp, stride=pitch), :]

        first = slab(0)
        prev = jnp.where(rowi == 0, 0.0, pltpu.roll(slab(p - 1), 1, axis=0))
        cur = first
        for n1 in range(p):
            if n1 < p - 1:
                nxt = slab(n1 + 1)
            else:
                nxt = jnp.where(rowi == hp - 1, 0.0, pltpu.roll(first, hp - 1, axis=0))
            col = n1 * c + t * LANES
            o_ref[:, col:col + LANES] = bias + prev * w0 + cur * w1 + nxt * w2
            prev, cur = cur, nxt


def _short_conv(hy_in, conv_w, conv_b, p):
    b, s, c3 = hy_in.shape
    c = c3 // 3
    nt = c // LANES
    x_spec = lambda t: pl.BlockSpec((None, s, LANES), lambda bi, j: (bi, 0, j * nt + t))
    pitch = p + 4
    return pl.pallas_call(
        functools.partial(_short_conv_kernel, p=p, pitch=pitch),
        grid=(b, 3),
        in_specs=[x_spec(t) for t in range(nt)] + [
            pl.BlockSpec((HY_SHORT, c), lambda bi, j: (0, j)),
            pl.BlockSpec((1, c), lambda bi, j: (0, j)),
        ],
        out_specs=pl.BlockSpec((None, None, s // p, p * c), lambda bi, j: (j, bi, 0, 0)),
        out_shape=jax.ShapeDtypeStruct((3, b, s // p, p * c), F32),
        scratch_shapes=[pltpu.VMEM((nt, (s // p) * pitch, LANES), F32)],
        compiler_params=_cparams("parallel", "parallel"),
        name="hy_short_conv",
    )(*([hy_in] * nt), conv_w, conv_b.reshape(1, c3))


def _filter_kernel(band_ref, w1_ref, b1_ref, f1_ref, w2_ref, b2_ref, f2_ref, w3_ref, delta_ref, o_ref,
                   *, seq):
    tr = o_ref.shape[1]
    th = tr // 2
    half = LANES // 2
    c = delta_ref.shape[-1]
    oc = HY_ORDER * c
    row = pl.program_id(0) * tr + lax.broadcasted_iota(jnp.int32, (th, LANES), 0)
    lane = lax.broadcasted_iota(jnp.int32, (th, LANES), 1)

    def position(n):
        return jnp.where(n < seq, n, 2 * seq - n).astype(F32)

    pos_top, pos_bot = position(row), position(row + th)
    pos = jnp.where(lane < half, pos_top, pos_bot)
    ang = (2.0 * math.pi * pos / seq) * band_ref[0:1, :] + band_ref[1:2, :]
    feat = lane % half
    z = jnp.where(feat == 0, pos / (seq - 1), jnp.where(feat <= 2 * HY_BANDS, jnp.cos(ang), 0.0))
    h = jnp.sin(f1_ref[...] * (jnp.dot(z, w1_ref[...], precision=HIGHEST, preferred_element_type=F32)
                               + b1_ref[...]))
    h = jnp.sin(f2_ref[...] * (jnp.dot(h, w2_ref[...], precision=HIGHEST, preferred_element_type=F32)
                               + b2_ref[...]))
    h = jnp.dot(h, w3_ref[...], precision=HIGHEST, preferred_element_type=F32)
    for part, (n, p) in enumerate(((row, pos_top), (row + th, pos_bot))):
        t_c = jnp.concatenate([p / (seq - 1)] * (c // LANES), axis=-1)
        window = jnp.exp(-t_c * delta_ref[...]) + HY_DECAY_SHIFT
        live = jnp.concatenate([n] * (c // LANES), axis=-1) != seq
        for o in range(HY_ORDER):
            lo = part * oc + o * c
            o_ref[o, part * th:(part + 1) * th, :] = jnp.where(live, h[:, lo:lo + c] * window, 0.0)


def _block_diag2(w):
    z = jnp.zeros_like(w)
    return jnp.concatenate([jnp.concatenate([w, z], axis=1), jnp.concatenate([z, w], axis=1)], axis=0)


def _hyena_filters(seq, w1, b1, f1, w2, b2, f2, w3, *, tr=512):
    emb, hid = w1.shape
    half = LANES // 2
    assert hid == half and emb <= half
    c = w3.shape[1] // (HY_ORDER * 2)
    n = 2 * seq
    bands = np.zeros((2, half), np.float32)
    bands[1, 1 + HY_BANDS:1 + 2 * HY_BANDS] = 0.5 * np.pi
    bvals = np.linspace(1e-4, HY_BANDS - 1, HY_BANDS, dtype=np.float32)
    bands[0, 1:1 + HY_BANDS] = bvals
    bands[0, 1 + HY_BANDS:1 + 2 * HY_BANDS] = bvals
    bands = np.concatenate([bands, bands], axis=1)
    max_decay = math.log(HY_DECAY_TARGET) / HY_FAST_DECAY_PCT
    min_decay = math.log(HY_DECAY_TARGET) / HY_SLOW_DECAY_PCT
    deltas = np.abs(np.linspace(min_decay, max_decay, c, dtype=np.float32)).reshape(1, c)
    twice = lambda a: jnp.concatenate([a, a]).reshape(1, LANES)
    w1d = _block_diag2(jnp.pad(w1, ((0, half - emb), (0, 0))))
    w3d = w3.reshape(hid, HY_ORDER, 2, c).transpose(2, 0, 1, 3).reshape(2, hid, HY_ORDER * c)
    w3d = jnp.stack([_block_diag2(w3d[0]), _block_diag2(w3d[1])])
    half_tiles = seq // tr
    kern = functools.partial(_filter_kernel, seq=seq)
    return pl.pallas_call(
        kern,
        grid=(n // tr,),
        in_specs=[
            _const_spec((2, LANES)),
            _const_spec((LANES, LANES)), _const_spec((1, LANES)), _const_spec((1, LANES)),
            _const_spec((LANES, LANES)), _const_spec((1, LANES)), _const_spec((1, LANES)),
            pl.BlockSpec((None, LANES, 2 * HY_ORDER * c), lambda i: (i // half_tiles, 0, 0)),
            _const_spec((1, c)),
        ],
        out_specs=pl.BlockSpec((HY_ORDER, tr, c), lambda i: (0, i, 0)),
        out_shape=jax.ShapeDtypeStruct((HY_ORDER, n, c), F32),
        compiler_params=_cparams("parallel"),
        name="hy_filter",
    )(jnp.asarray(bands), w1d, twice(b1), twice(f1), _block_diag2(w2), twice(b2), twice(f2), w3d,
      jnp.asarray(deltas))


def _dft_tables(p):
    idx = np.arange(p)
    ang = 2.0 * np.pi * np.outer(idx, idx) / p
    fr, fi = np.cos(ang), -np.sin(ang)
    tang = 2.0 * np.pi * np.outer(idx, idx) / (p * p)
    twr, twi = np.cos(tang), -np.sin(tang)
    return fr, fi, twr, twi


def _split_bf16(x):
    hi = x.astype(BF16)
    return hi, (x - hi.astype(F32)).astype(BF16)


def _dft_dot(mat, data, *, split):
    dot = functools.partial(jnp.dot, preferred_element_type=F32)
    if not split:
        return dot(mat.astype(BF16), data.astype(BF16))
    mh, ml = _split_bf16(mat)
    dh, dl = _split_bf16(data)
    return dot(mh, dh) + (dot(mh, dl) + dot(ml, dh))


def _stage_a_kernel(x_ref, m_ref, o_ref, *, split):
    x = jnp.concatenate([x_ref[0], x_ref[1]], axis=0)
    o_ref[0] = _dft_dot(m_ref[...], x, split=split).astype(o_ref.dtype)


def _stage_a(x4, idx, mat, *, split, ct=2048):
    _, g2, hp, cols = x4.shape
    rows = mat.shape[0]
    return pl.pallas_call(
        functools.partial(_stage_a_kernel, split=split),
        grid=(g2 // 2, cols // ct),
        in_specs=[
            pl.BlockSpec((None, 2, hp, ct), lambda g, j: (idx, g, 0, j)),
            _const_spec(mat.shape),
        ],
        out_specs=pl.BlockSpec((1, rows, ct), lambda g, j: (g, 0, j)),
        out_shape=jax.ShapeDtypeStruct((g2 // 2, rows, cols), F32 if split else BF16),
        compiler_params=_cparams("parallel", "parallel"),
        name="hy_dft_a",
    )(x4, mat)


def _fwd_matrix(fr, fi, twr, twi):
    gr = fr * twr - fi * twi
    gi = fr * twi + fi * twr
    return jnp.concatenate([jnp.concatenate([gr, -gi], axis=1), jnp.concatenate([gi, gr], axis=1)], axis=0)


def _stack_pairs(a_ref, j):
    g = a_ref.shape[0]
    re = jnp.concatenate([a_ref[i, 0, j] for i in range(g)], axis=-1)
    im = jnp.concatenate([a_ref[i, 1, j] for i in range(g)], axis=-1)
    return jnp.concatenate([re, im], axis=0)


def _stage_c_kernel(a_ref, fr_ref, fi_ref, twr_ref, twi_ref, o_ref):
    g, _, kk, p, c = a_ref.shape
    fr, fi = fr_ref[...], fi_ref[...]
    xs = [_dft_dot(_fwd_matrix(fr, fi, twr_ref[j], twi_ref[j]), _stack_pairs(a_ref, j), split=True)
          for j in range(kk)]
    for j, x in enumerate(xs):
        for i in range(g):
            o_ref[i, 0, j] = x[:p, i * c:(i + 1) * c]
            o_ref[i, 1, j] = x[p:, i * c:(i + 1) * c]


def _conv_core_kernel(a_ref, kh_ref, fr_ref, fi_ref, twr_ref, twi_ref, twbr_ref, twbi_ref, o_ref):
    g, _, kk, p, c = a_ref.shape
    fr, fi = fr_ref[...], fi_ref[...]
    xs = [_dft_dot(_fwd_matrix(fr, fi, twr_ref[j], twi_ref[j]), _stack_pairs(a_ref, j), split=False)
          for j in range(kk)]
    ys = []
    for j, x in enumerate(xs):
        xr, xi = x[:p], x[p:]
        kr = jnp.concatenate([kh_ref[0, j]] * g, axis=-1)
        ki = jnp.concatenate([kh_ref[1, j]] * g, axis=-1)
        ys.append(jnp.concatenate([xr * kr - xi * ki, xr * ki + xi * kr], axis=0))
    rs = []
    for j, y in enumerate(ys):
        br = fr * twbr_ref[j] - fi * twbi_ref[j]
        bi = -(fr * twbi_ref[j] + fi * twbr_ref[j])
        bs = jnp.concatenate([jnp.concatenate([br, -bi], axis=1), jnp.concatenate([bi, br], axis=1)], axis=0)
        rs.append(_dft_dot(bs, y, split=False).astype(o_ref.dtype))
    for j, r in enumerate(rs):
        for i in range(g):
            o_ref[i, 0, j] = r[:p, i * c:(i + 1) * c]
            o_ref[i, 1, j] = r[p:, i * c:(i + 1) * c]


def _tw_specs(p, kk):
    row = pl.BlockSpec((kk, 1, p), lambda i: (i, 0, 0))
    return row


def _stage_c(a5, tabs, *, kk=4):
    g, _, p, _, c = a5.shape
    fr, fi, twr, twi = tabs
    blk = pl.BlockSpec((g, 2, kk, p, c), lambda i: (0, 0, i, 0, 0))
    return pl.pallas_call(
        _stage_c_kernel,
        grid=(p // kk,),
        in_specs=[blk, _const_spec((p, p)), _const_spec((p, p)), _tw_specs(p, kk), _tw_specs(p, kk)],
        out_specs=blk,
        out_shape=jax.ShapeDtypeStruct(a5.shape, F32),
        compiler_params=_cparams("parallel"),
        name="hy_dft_c",
    )(a5, fr, fi, twr.reshape(p, 1, p), twi.reshape(p, 1, p))


def _conv_core(a5, khat, order, tabs, twb, *, kk=8):
    g, _, p, _, c = a5.shape
    fr, fi, twr, twi = tabs
    twbr, twbi = twb
    blk = pl.BlockSpec((g, 2, kk, p, c), lambda i: (0, 0, i, 0, 0))
    return pl.pallas_call(
        _conv_core_kernel,
        grid=(p // kk,),
        in_specs=[
            blk,
            pl.BlockSpec((None, 2, kk, p, c), lambda i: (order, 0, i, 0, 0)),
            _const_spec((p, p)), _const_spec((p, p)), _tw_specs(p, kk), _tw_specs(p, kk),
            pl.BlockSpec((kk, p, p), lambda i: (i, 0, 0)),
            pl.BlockSpec((kk, p, p), lambda i: (i, 0, 0)),
        ],
        out_specs=blk,
        out_shape=jax.ShapeDtypeStruct(a5.shape, BF16),
        compiler_params=_cparams("parallel"),
        name="hy_conv_core",
    )(a5, khat, fr, fi, twr.reshape(p, 1, p), twi.reshape(p, 1, p), twbr, twbi)


def _stage_b_kernel(r_ref, m_ref, u_ref, gate_ref, skip_ref, ma_ref, o_ref, *next_ref):
    y = _dft_dot(m_ref[...], r_ref[0], split=False)
    hp = u_ref.shape[1]
    skip = skip_ref[...]
    z0 = gate_ref[0] * (y[:hp] + skip * u_ref[0])
    z1 = gate_ref[1] * (y[hp:] + skip * u_ref[1])
    o_ref[0] = z0
    o_ref[1] = z1
    if next_ref:
        x = jnp.concatenate([z0, z1], axis=0)
        next_ref[0][0] = _dft_dot(ma_ref[...], x, split=False).astype(next_ref[0].dtype)


def _stage_b(r3, mat, u4, u_idx, gate4, gate_idx, skip_row, mat_a, *, emit_next, ct=2048):
    g, rows, cols = r3.shape
    hp = mat.shape[0] // 2
    nb = u4.shape[1]
    out_specs = [pl.BlockSpec((2, hp, ct), lambda gi, j: (gi, 0, j))]
    out_shape = [jax.ShapeDtypeStruct((nb, hp, cols), F32)]
    if emit_next:
        out_specs.append(pl.BlockSpec((1, mat_a.shape[0], ct), lambda gi, j: (gi, 0, j)))
        out_shape.append(jax.ShapeDtypeStruct((g, mat_a.shape[0], cols), BF16))
    return pl.pallas_call(
        _stage_b_kernel,
        grid=(g, cols // ct),
        in_specs=[
            pl.BlockSpec((1, rows, ct), lambda gi, j: (gi, 0, j)),
            _const_spec(mat.shape),
            pl.BlockSpec((None, 2, hp, ct), lambda gi, j: (u_idx, gi, 0, j)),
            pl.BlockSpec((None, 2, hp, ct), lambda gi, j: (gate_idx, gi, 0, j)),
            pl.BlockSpec((1, ct), lambda gi, j: (0, j)),
            _const_spec(mat_a.shape),
        ],
        out_specs=out_specs,
        out_shape=out_shape,
        compiler_params=_cparams("parallel", "parallel"),
        name="hy_dft_b",
    )(r3, mat, u4, gate4, skip_row, mat_a)


def _hyena(hy_in, conv_w, conv_b, w1, b1, f1, w2, b2, f2, w3, skip):
    b, s, c3 = hy_in.shape
    c = c3 // 3
    p = math.isqrt(2 * s)
    assert p * p == 2 * s and b % 2 == 0
    hp = p // 2
    cols = p * c
    fr, fi, twr, twi = _dft_tables(p)
    scale = 1.0 / (p * p)
    mat_a = np.block([[fr[:, :hp], -fi[:, :hp]], [fi[:, :hp], fr[:, :hp]]]).astype(np.float32)
    mat_k = np.concatenate([fr, fi], axis=0).astype(np.float32)
    br, bi = fr[:hp] * scale, -fi[:hp] * scale
    mat_b = np.block([[br, -bi], [bi, br]]).astype(np.float32)
    tabs = tuple(jnp.asarray(t.astype(np.float32)) for t in (fr, fi, twr, twi))
    twb = tuple(jnp.broadcast_to(t[:, :, None], (p, p, p)) for t in tabs[2:])

    uvx = _short_conv(hy_in, conv_w, conv_b, p)
    kern = _hyena_filters(s, w1, b1, f1, w2, b2, f2, w3)
    ka = _stage_a(kern.reshape(1, HY_ORDER * 2, hp, cols), 0, jnp.asarray(mat_k), split=True)
    khat = _stage_c(ka.reshape(HY_ORDER, 2, p, p, c), tabs)

    skip_rows = jnp.tile(skip, (1, p))
    g = b // 2

    ma, mb = jnp.asarray(mat_a), jnp.asarray(mat_b)

    def long_conv(a, u4, u_idx, gate_idx, order, emit_next):
        r = _conv_core(a.reshape(g, 2, p, p, c), khat, order, tabs, twb)
        return _stage_b(r.reshape(g, 2 * p, cols), mb, u4, u_idx, uvx, gate_idx,
                        skip_rows[order:order + 1], ma, emit_next=emit_next)

    a1 = _stage_a(uvx, 0, ma, split=False)
    z, a2 = long_conv(a1, uvx, 0, 1, 0, True)
    (y,) = long_conv(a2, z[None], 0, 2, 1, False)
    return y.reshape(b, s, c)


def kernel(x, ffn1_norm, ffn1_w_gate, ffn1_w_up, ffn1_w_down, mix_norm, w_in, hy_conv_w, hy_conv_b, hy_filt_w1, hy_filt_b1, hy_filt_freq1, hy_filt_w2, hy_filt_b2, hy_filt_freq2, hy_filt_w3, hy_skip, hy_out_norm, mla_q_norm, mla_w_uq, mla_kv_norm, mla_w_ukv, mla_out_norm, gla_w_gate_fwd, gla_b_gate_fwd, gla_w_gate_bwd, gla_b_gate_bwd, gla_head_norm, w_out, ffn2_norm, ffn2_w_gate, ffn2_w_up, ffn2_w_down, final_norm):
    b, s, d = x.shape
    depth = ffn1_norm.shape[0]
    t = b * s
    for l in range(depth):
        x2 = _ffn(x.reshape(t, d), ffn1_norm[l], ffn1_w_gate[l], ffn1_w_up[l], ffn1_w_down[l], final_norm,
                  final=False)
        (hy_in, q, k, v, gq, gk, gv, gf, gb, gr) = _in_proj(
            x2.reshape(b, s, d), mix_norm[l], w_in[l], mla_q_norm[l], mla_w_uq[l], mla_kv_norm[l],
            mla_w_ukv[l], gla_w_gate_fwd[l], gla_b_gate_fwd[l], gla_w_gate_bwd[l], gla_b_gate_bwd[l])
        y_hy = _hyena(hy_in, hy_conv_w[l], hy_conv_b[l], hy_filt_w1[l], hy_filt_b1[l], hy_filt_freq1[l],
                      hy_filt_w2[l], hy_filt_b2[l], hy_filt_freq2[l], hy_filt_w3[l], hy_skip[l])
        o_mla = _mla_attention(q, k, v)
        y_gla = _gla(gq, gk, gv, gf, gb, gr, gla_head_norm[l])
        mix = (y_hy.reshape(t, -1), o_mla.reshape(t, -1), y_gla.reshape(t, -1), hy_out_norm[l],
               mla_out_norm[l], w_out[l])
        x = _ffn(x2, ffn2_norm[l], ffn2_w_gate[l], ffn2_w_up[l], ffn2_w_down[l], final_norm,
                 final=(l == depth - 1), mix=mix).reshape(b, s, d)
    return x
```

```python
import functools
import math

import jax
import jax.numpy as jnp
import numpy as np
from jax import lax
from jax.experimental import pallas as pl
from jax.experimental.pallas import tpu as pltpu

F32 = jnp.float32
BF16 = jnp.bfloat16
HIGHEST = lax.Precision.HIGHEST

NORM_EPS = 1e-6
HY_ORDER = 2
HY_SHORT = 3
HY_BANDS = 16
HY_FAST_DECAY_PCT = 0.3
HY_SLOW_DECAY_PCT = 1.5
HY_DECAY_TARGET = 1e-2
HY_DECAY_SHIFT = 0.0
MLA_HEADS = 4
MLA_QK_NOPE = 128
MLA_QK_ROPE = 64
MLA_V_HEAD = 128
ROPE_BASE = 10000.0
GLA_HEADS = 4
GLA_GATE_RANK = 16
GLA_GATE_NORM = 16.0

LANES = 128
GLA_CHUNK = 64
GLA_SUB = 16
GLA_FACTORED_MIN_GATE = -1.0
VMEM_LIMIT = 56 * 1024 * 1024

NEG_BIG = -1e30


def _cparams(*sem):
    return pltpu.CompilerParams(dimension_semantics=sem, vmem_limit_bytes=VMEM_LIMIT)


def _rms(x, g):
    ms = jnp.mean(x * x, axis=-1, keepdims=True)
    return x * lax.rsqrt(ms + NORM_EPS) * g


def _const_spec(shape):
    nd = len(shape)
    return pl.BlockSpec(shape, lambda *_: (0,) * nd)


def _ffn_kernel(x_ref, g_ref, wg_ref, wu_ref, wd_ref, fg_ref, o_ref, *, ff_chunk, final):
    _ffn_body(x_ref[...], g_ref, wg_ref, wu_ref, wd_ref, fg_ref, o_ref, ff_chunk=ff_chunk, final=final)


def _mix_ffn_kernel(x_ref, hy_ref, mla_ref, gla_ref, ghy_ref, gmla_ref, wout_ref,
                    g_ref, wg_ref, wu_ref, wd_ref, fg_ref, o_ref, *, ff_chunk, final):
    y_hy = _rms(hy_ref[...], ghy_ref[...]).astype(BF16)
    y_mla = _rms(mla_ref[...], gmla_ref[...]).astype(BF16)
    cat = jnp.concatenate([y_hy, y_mla, gla_ref[...]], axis=-1)
    x = x_ref[...] + jnp.dot(cat, wout_ref[...], preferred_element_type=F32)
    _ffn_body(x, g_ref, wg_ref, wu_ref, wd_ref, fg_ref, o_ref, ff_chunk=ff_chunk, final=final)


def _ffn_body(x, g_ref, wg_ref, wu_ref, wd_ref, fg_ref, o_ref, *, ff_chunk, final):
    xn = _rms(x, g_ref[...]).astype(BF16)
    d_ff = wg_ref.shape[1]
    acc = jnp.zeros_like(x)
    for c in range(0, d_ff, ff_chunk):
        gate = jnp.dot(xn, wg_ref[:, c:c + ff_chunk], preferred_element_type=F32)
        up = jnp.dot(xn, wu_ref[:, c:c + ff_chunk], preferred_element_type=F32)
        h = (gate * jax.nn.sigmoid(gate) * up).astype(BF16)
        acc = acc + jnp.dot(h, wd_ref[c:c + ff_chunk, :], preferred_element_type=F32)
    y = x + 0.5 * acc
    if final:
        y = _rms(y, fg_ref[...])
    o_ref[...] = y


def _resident_spec(shape):
    nd = len(shape)
    return pl.BlockSpec(shape, lambda *_: (0,) * nd, pipeline_mode=pl.Buffered(1))


def _ffn(x2d, norm_g, w_gate, w_up, w_down, final_g, *, final, mix=None, tm=512, ff_chunk=256):
    t, d = x2d.shape
    d_ff = w_gate.shape[1]
    tok = lambda w: pl.BlockSpec((tm, w), lambda i: (i, 0))
    ffn_specs = [_const_spec((1, d)), _resident_spec((d, d_ff)), _resident_spec((d, d_ff)),
                 _resident_spec((d_ff, d)), _const_spec((1, d))]
    ffn_args = (norm_g.reshape(1, d), w_gate.astype(BF16), w_up.astype(BF16), w_down.astype(BF16),
                final_g.reshape(1, d))
    if mix is None:
        kern, name = _ffn_kernel, "ffn"
        in_specs = [tok(d)] + ffn_specs
        args = (x2d,) + ffn_args
    else:
        y_hy, o_mla, y_gla, g_hy, g_mla, w_out = mix
        kern, name = _mix_ffn_kernel, "mix_ffn"
        in_specs = [tok(d), tok(y_hy.shape[1]), tok(o_mla.shape[1]), tok(y_gla.shape[1]),
                    _const_spec((1, y_hy.shape[1])), _const_spec((1, o_mla.shape[1])),
                    _resident_spec(w_out.shape)] + ffn_specs
        args = (x2d, y_hy, o_mla, y_gla, g_hy.reshape(1, -1), g_mla.reshape(1, -1),
                w_out.astype(BF16)) + ffn_args
    return pl.pallas_call(
        functools.partial(kern, ff_chunk=ff_chunk, final=final),
        grid=(t // tm,),
        in_specs=in_specs,
        out_specs=tok(d),
        out_shape=jax.ShapeDtypeStruct((t, d), F32),
        compiler_params=_cparams("parallel"),
        name=name,
    )(*args)


def _log_sigmoid(z):
    return jnp.minimum(z, 0.0) - jnp.log(1.0 + jnp.exp(-jnp.abs(z)))


def _in_proj_kernel(x_ref, g_ref, win_ref, qn_ref, wuq_ref, kvn_ref, wukv_ref, cos_ref, sin_ref,
                    wgate_ref, bgate_ref,
                    hy_ref, q_ref, k_ref, v_ref, gq_ref, gk_ref, gv_ref, gf_ref, gb_ref, gr_ref,
                    *, off, q_scale, gla_q_scale):
    x = x_ref[0]
    xn = _rms(x, g_ref[...]).astype(BF16)
    proj = jnp.dot(xn, win_ref[...], preferred_element_type=F32)

    def blk(name):
        lo, hi = off[name]
        return proj[:, lo:hi]

    hy_ref[0] = blk("hy")

    qn = _rms(blk("cq"), qn_ref[...]).astype(BF16)
    qf = jnp.dot(qn, wuq_ref[...], preferred_element_type=F32)
    kvn = _rms(blk("ckv"), kvn_ref[...]).astype(BF16)
    kvf = jnp.dot(kvn, wukv_ref[...], preferred_element_type=F32)
    cos = cos_ref[...]
    sin = sin_ref[...]
    k_rope = blk("kr") * cos + blk("kr_rot") * sin
    ones = jnp.ones_like(k_rope)
    qw = 3 * LANES
    kvw = 2 * LANES
    for h in range(MLA_HEADS):
        q_nope = qf[:, h * qw:h * qw + LANES]
        q_rope = (qf[:, h * qw + LANES:h * qw + 2 * LANES] * cos
                  + qf[:, h * qw + 2 * LANES:(h + 1) * qw] * sin)
        q_ref[0, h] = (jnp.concatenate([q_nope, q_rope], axis=-1) * q_scale).astype(BF16)
        k_ref[0, h] = jnp.concatenate([kvf[:, h * kvw:h * kvw + LANES], k_rope], axis=-1).astype(BF16)
        v_ref[0, h] = jnp.concatenate([kvf[:, h * kvw + LANES:(h + 1) * kvw], ones], axis=-1).astype(BF16)

    gq_ref[0] = blk("gq") * gla_q_scale
    gk_ref[0] = blk("gk")
    gv_ref[0] = blk("gv")
    z = jnp.dot(blk("gate"), wgate_ref[...], precision=HIGHEST, preferred_element_type=F32) + bgate_ref[...]
    gates = _log_sigmoid(z) / GLA_GATE_NORM
    gk_w = gf_ref.shape[-1]
    gf_ref[0] = gates[:, :gk_w]
    gb_ref[0] = gates[:, gk_w:]
    gr_ref[0] = blk("gr")


def _rope_rot_cols(w):
    half = w.shape[-1] // 2
    return jnp.concatenate([-w[..., half:], w[..., :half]], axis=-1)


def _pad_cols(w, width):
    return jnp.pad(w, ((0, 0), (0, width - w.shape[-1])))


def _in_proj(x, mix_g, w_in, q_norm, w_uq, kv_norm, w_ukv, w_gf, b_gf, w_gb, b_gb, *, tm=512):
    b, s, d = x.shape
    hy_in = 3 * (d // 4)
    q_lora = d // 4
    kv_lora = d // 8
    gla_w = d // 4
    gla_key = gla_w // 2
    rope = MLA_QK_ROPE
    splits = (hy_in, q_lora, kv_lora, rope, gla_key, gla_key, gla_w, GLA_GATE_RANK, GLA_GATE_RANK, gla_w)
    offs = np.concatenate([[0], np.cumsum(splits)])
    (w_hy, w_cq, w_ckv, w_kr, w_gq, w_gk, w_gv, w_gfl, w_gbl, w_gr) = [
        w_in[:, offs[i]:offs[i + 1]] for i in range(len(splits))]
    pieces = [
        ("hy", w_hy), ("cq", w_cq), ("ckv", w_ckv),
        ("kr", _pad_cols(w_kr, LANES)), ("kr_rot", _pad_cols(_rope_rot_cols(w_kr), LANES)),
        ("gq", w_gq), ("gk", w_gk), ("gv", w_gv),
        ("gate", _pad_cols(jnp.concatenate([w_gfl, w_gbl], axis=1), LANES)),
        ("gr", w_gr),
    ]
    off = {}
    pos = 0
    for name, w in pieces:
        off[name] = (pos, pos + w.shape[1])
        pos += w.shape[1]
    win_r = jnp.concatenate([w for _, w in pieces], axis=1).astype(BF16)
    n_in = win_r.shape[1]

    dqk = MLA_QK_NOPE + MLA_QK_ROPE
    wq = w_uq.reshape(q_lora, MLA_HEADS, dqk)
    wq_nope = wq[..., :MLA_QK_NOPE]
    wq_rope = wq[..., MLA_QK_NOPE:]
    zpad = jnp.zeros((q_lora, MLA_HEADS, LANES - rope), F32)
    wuq_r = jnp.concatenate([wq_nope, wq_rope, zpad, _rope_rot_cols(wq_rope), zpad], axis=-1)
    wuq_r = wuq_r.reshape(q_lora, MLA_HEADS * 3 * LANES).astype(BF16)
    wukv_r = w_ukv.astype(BF16)

    half = rope // 2
    inv = ROPE_BASE ** (-jnp.arange(half, dtype=F32) * 2.0 / rope)
    ang = jnp.arange(s, dtype=F32)[:, None] * inv[None, :]
    zt = jnp.zeros((s, LANES - rope), F32)
    cos_t = jnp.concatenate([jnp.cos(ang), jnp.cos(ang), zt], axis=1)
    sin_t = jnp.concatenate([jnp.sin(ang), jnp.sin(ang), zt], axis=1)

    wgate = jnp.zeros((LANES, 2 * gla_key), F32)
    wgate = wgate.at[:GLA_GATE_RANK, :gla_key].set(w_gf)
    wgate = wgate.at[GLA_GATE_RANK:2 * GLA_GATE_RANK, gla_key:].set(w_gb)
    bgate = jnp.concatenate([b_gf, b_gb]).reshape(1, 2 * gla_key)

    kern = functools.partial(_in_proj_kernel, off=off, q_scale=float(dqk) ** -0.5 * math.log2(math.e),
                             gla_q_scale=float(gla_key // GLA_HEADS) ** -0.5)
    tok = lambda w: pl.BlockSpec((1, tm, w), lambda bi, i: (bi, i, 0))
    head = lambda w: pl.BlockSpec((1, MLA_HEADS, tm, w), lambda bi, i: (bi, 0, i, 0))
    outs = pl.pallas_call(
        kern,
        grid=(b, s // tm),
        in_specs=[
            tok(d),
            _const_spec((1, d)),
            _const_spec((d, n_in)),
            _const_spec((1, q_lora)),
            _const_spec((q_lora, MLA_HEADS * 3 * LANES)),
            _const_spec((1, kv_lora)),
            _const_spec((kv_lora, MLA_HEADS * 2 * LANES)),
            pl.BlockSpec((tm, LANES), lambda bi, i: (i, 0)),
            pl.BlockSpec((tm, LANES), lambda bi, i: (i, 0)),
            _const_spec((LANES, 2 * gla_key)),
            _const_spec((1, 2 * gla_key)),
        ],
        out_specs=[
            tok(hy_in),
            head(2 * LANES), head(2 * LANES), head(2 * LANES),
            tok(gla_key), tok(gla_key), tok(gla_w), tok(gla_key), tok(gla_key), tok(gla_w),
        ],
        out_shape=[
            jax.ShapeDtypeStruct((b, s, hy_in), F32),
            jax.ShapeDtypeStruct((b, MLA_HEADS, s, 2 * LANES), BF16),
            jax.ShapeDtypeStruct((b, MLA_HEADS, s, 2 * LANES), BF16),
            jax.ShapeDtypeStruct((b, MLA_HEADS, s, 2 * LANES), BF16),
            jax.ShapeDtypeStruct((b, s, gla_key), F32),
            jax.ShapeDtypeStruct((b, s, gla_key), F32),
            jax.ShapeDtypeStruct((b, s, gla_w), F32),
            jax.ShapeDtypeStruct((b, s, gla_key), F32),
            jax.ShapeDtypeStruct((b, s, gla_key), F32),
            jax.ShapeDtypeStruct((b, s, gla_w), F32),
        ],
        compiler_params=_cparams("parallel", "parallel"),
        name="in_proj",
    )(x, mix_g.reshape(1, d), win_r, q_norm.reshape(1, -1), wuq_r, kv_norm.reshape(1, -1), wukv_r,
      cos_t, sin_t, wgate, bgate)
    return outs


def _attn_kernel(q_ref, k_ref, v_ref, o_ref, s0_ref, s1_ref, p_ref, acc_ref, m_ref, alpha_ref, *, tk, rb):
    q = q_ref[...]
    tq = q.shape[0]
    nk = k_ref.shape[0] // tk
    dv = o_ref.shape[-1]

    def scores(j, dst_ref):
        start = pl.multiple_of(j * tk, tk)
        dst_ref[...] = lax.dot_general(q, k_ref[pl.ds(start, tk), :], (((1,), (1,)), ((), ())),
                                       preferred_element_type=F32)

    def consume(j, cur_ref, nxt_ref):
        if nxt_ref is not None:
            scores(j + 1, nxt_ref)
        for r in range(tq // rb):
            rows = slice(r * rb, (r + 1) * rb)
            tiles = [cur_ref[rows, t * LANES:(t + 1) * LANES] for t in range(tk // LANES)]
            mx = functools.reduce(jnp.maximum, tiles)
            m_old = m_ref[rows, :]
            m_new = jnp.maximum(m_old, jnp.broadcast_to(jnp.max(mx, axis=-1, keepdims=True), m_old.shape))
            alpha_ref[rows, :] = jnp.exp2(m_old - m_new)
            for t, s_t in enumerate(tiles):
                p_ref[rows, t * LANES:(t + 1) * LANES] = jnp.exp2(s_t - m_new).astype(BF16)
            m_ref[rows, :] = m_new
        start = pl.multiple_of(j * tk, tk)
        pv = jnp.dot(p_ref[...], v_ref[pl.ds(start, tk), :], preferred_element_type=F32)
        alpha = alpha_ref[...]
        for t in range(acc_ref.shape[1] // LANES):
            cols = slice(t * LANES, (t + 1) * LANES)
            acc_ref[:, cols] = acc_ref[:, cols] * alpha + pv[:, cols]

    m_ref[...] = jnp.full(m_ref.shape, -jnp.inf, F32)
    acc_ref[...] = jnp.zeros(acc_ref.shape, F32)
    scores(0, s0_ref)

    def pair(i, carry):
        consume(2 * i, s0_ref, s1_ref)
        consume(2 * i + 1, s1_ref, s0_ref)
        return carry

    lax.fori_loop(0, nk // 2 - 1, pair, 0)
    consume(nk - 2, s0_ref, s1_ref)
    consume(nk - 1, s1_ref, None)
    acc = acc_ref[...]
    o_ref[...] = acc[:, :dv] / acc[:, dv:]


def _mla_attention(q, k, v, *, tq=1024, tk=2048, rb=64):
    b, h, s, dq = q.shape
    dv = MLA_V_HEAD
    tk = min(tk, s // 2)
    assert (s // tk) % 2 == 0
    kern = functools.partial(_attn_kernel, tk=tk, rb=rb)
    return pl.pallas_call(
        kern,
        grid=(b, h, s // tq),
        in_specs=[
            pl.BlockSpec((None, None, tq, dq), lambda bi, hi, i: (bi, hi, i, 0)),
            pl.BlockSpec((None, None, s, dq), lambda bi, hi, i: (bi, hi, 0, 0)),
            pl.BlockSpec((None, None, s, 2 * dv), lambda bi, hi, i: (bi, hi, 0, 0)),
        ],
        out_specs=pl.BlockSpec((None, tq, dv), lambda bi, hi, i: (bi, i, hi)),
        out_shape=jax.ShapeDtypeStruct((b, s, h * dv), F32),
        scratch_shapes=[
            pltpu.VMEM((tq, tk), F32), pltpu.VMEM((tq, tk), F32), pltpu.VMEM((tq, tk), BF16),
            pltpu.VMEM((tq, 2 * dv), F32), pltpu.VMEM((tq, LANES), F32), pltpu.VMEM((tq, LANES), F32),
        ],
        compiler_params=_cparams("parallel", "parallel", "parallel"),
        name="mla_attn",
    )(q, k, v)


def _split3(x):
    x0 = x.astype(BF16)
    r1 = x - x0.astype(F32)
    x1 = r1.astype(BF16)
    x2 = (r1 - x1.astype(F32)).astype(BF16)
    return x0, x1, x2


def _chunk_cumsum(tri, g):
    g0, g1, g2 = _split3(g)
    dot = functools.partial(jnp.dot, preferred_element_type=F32)
    return dot(tri, g0) + (dot(tri, g1) + dot(tri, g2))


def _gla_intra(q, k, v, b2f, b2b, e3, hmask, vmask):
    c = q.shape[0]
    jj = lax.broadcasted_iota(jnp.int32, (GLA_SUB, GLA_SUB, LANES), 0)
    ii = lax.broadcasted_iota(jnp.int32, (GLA_SUB, GLA_SUB, LANES), 1)
    lower = jj <= ii
    row = lax.broadcasted_iota(jnp.int32, (c, LANES), 0)
    v16 = v.astype(BF16)
    outs = []
    for sb in range(c // GLA_SUB):
        r0, r1 = sb * GLA_SUB, (sb + 1) * GLA_SUB
        qs, ks, vs, bf, bb = q[r0:r1], k[r0:r1], v[r0:r1], b2f[r0:r1], b2b[r0:r1]
        d = jnp.where(lower, bf[None, :, :] - bf[:, None, :], bb[None, :, :] - bb[:, None, :])
        p = (qs[None, :, :] * ks[:, None, :]) * jnp.exp2(d)
        w = jnp.dot(p.reshape(GLA_SUB * GLA_SUB, LANES).astype(BF16), e3, preferred_element_type=F32)
        o_sb = jnp.sum(w.reshape(GLA_SUB, GLA_SUB, w.shape[-1]) * vs[:, None, :], axis=0)
        q_parts, k_parts = [], []
        if r0 > 0:
            ref = b2f[r0 - 1:r0]
            q_parts.append(qs * jnp.exp2(bf - ref))
            k_parts.append(k * jnp.exp2(jnp.where(row < r0, ref - b2f, NEG_BIG)))
        if r1 < c:
            ref = b2b[r1:r1 + 1]
            q_parts.append(qs * jnp.exp2(bb - ref))
            k_parts.append(k * jnp.exp2(jnp.where(row >= r1, ref - b2b, NEG_BIG)))
        qt = jnp.concatenate(q_parts, axis=-1)
        kt = jnp.concatenate(k_parts, axis=-1).astype(BF16)
        hm = jnp.concatenate([hmask] * len(q_parts), axis=-1)
        q_stack = jnp.concatenate([qt * hm[hh:hh + 1] for hh in range(GLA_HEADS)], axis=0).astype(BF16)
        a = lax.dot_general(q_stack, kt, (((1,), (1,)), ((), ())), preferred_element_type=F32)
        o_all = jnp.dot(a.astype(BF16), v16, preferred_element_type=F32)
        for hh in range(GLA_HEADS):
            o_sb = o_sb + o_all[hh * GLA_SUB:(hh + 1) * GLA_SUB] * vmask[hh:hh + 1]
        outs.append(o_sb)
    return jnp.concatenate(outs, axis=0)


def _gla_state_kernel(kf_ref, vf_ref, gf_ref, kb_ref, vb_ref, gb_ref, tril_ref, triu_ref, stmask_ref,
                      sf_ref, sb_ref, stf_ref, stb_ref):
    @pl.when(pl.program_id(1) == 0)
    def _():
        stf_ref[...] = jnp.zeros_like(stf_ref)
        stb_ref[...] = jnp.zeros_like(stb_ref)

    nc = kf_ref.shape[0] // GLA_CHUNK
    stmask = stmask_ref[...]
    chunks = [slice(c * GLA_CHUNK, (c + 1) * GLA_CHUNK) for c in range(nc)]
    tn = (((0,), (0,)), ((), ()))

    for reverse in (False, True):
        if reverse:
            k_ref, v_ref, g_ref, s_ref, st_ref, tri = kb_ref, vb_ref, gb_ref, sb_ref, stb_ref, triu_ref[...]
        else:
            k_ref, v_ref, g_ref, s_ref, st_ref, tri = kf_ref, vf_ref, gf_ref, sf_ref, stf_ref, tril_ref[...]
        g, k, v16 = g_ref[...], k_ref[...], v_ref[...].astype(BF16)
        beta = [_chunk_cumsum(tri, g[r]) for r in chunks]
        tot = [b[0:1] if reverse else b[GLA_CHUNK - 1:GLA_CHUNK] for b in beta]
        kd = [(k[r] * jnp.exp(t - b)).astype(BF16) for r, t, b in zip(chunks, tot, beta)]
        ut = [lax.dot_general(v16[r], x, tn, preferred_element_type=F32) for r, x in zip(chunks, kd)]
        dec = [jnp.exp(t) for t in tot]
        st = st_ref[...]
        for c in (reversed(range(nc)) if reverse else range(nc)):
            s_ref[c] = st.astype(BF16)
            st = st * dec[c] + ut[c] * stmask
        st_ref[...] = st


def _gla_intra_factored(qf, qb, k, v16, b2f, b2b, hmask, vmask, lower):
    nt = (((1,), (1,)), ((), ()))
    n = len(k)
    c = k[0].shape[0]

    def head_rows(x):
        return jnp.concatenate([x * hmask[hh:hh + 1] for hh in range(GLA_HEADS)], axis=0).astype(BF16)

    kf = [(k[i] * jnp.exp2(-b2f[i])).astype(BF16) for i in range(n)]
    kb = [(k[i] * jnp.exp2(-b2b[i])).astype(BF16) for i in range(n)]
    hf = [head_rows(x) for x in qf]
    hb = [head_rows(x) for x in qb]
    a_f = [lax.dot_general(hf[i], kf[i], nt, preferred_element_type=F32) for i in range(n)]
    a_b = [lax.dot_general(hb[i], kb[i], nt, preferred_element_type=F32) for i in range(n)]
    a = [jnp.where(lower, a_f[i], a_b[i]).astype(BF16) for i in range(n)]
    o_all = [jnp.dot(a[i], v16[i], preferred_element_type=F32) for i in range(n)]
    outs = []
    for i in range(n):
        o = o_all[i][0:c] * vmask[0:1]
        for hh in range(1, GLA_HEADS):
            o = o + o_all[i][hh * c:(hh + 1) * c] * vmask[hh:hh + 1]
        outs.append(o)
    return outs


def _gla_out_kernel(q_ref, k_ref, v_ref, gf_ref, gb_ref, r_ref, sf_ref, sb_ref,
                    tril_ref, triu_ref, e3_ref, hmask_ref, vmask_ref, seg_ref, gn_ref, o_ref):
    nc = q_ref.shape[0] // GLA_CHUNK
    tril, triu = tril_ref[...], triu_ref[...]
    e3, hmask, vmask = e3_ref[...], hmask_ref[...], vmask_ref[...]
    seg, gn = seg_ref[...], gn_ref[...]
    log2e = math.log2(math.e)
    ri = lax.broadcasted_iota(jnp.int32, (GLA_HEADS * GLA_CHUNK, GLA_CHUNK), 0) % GLA_CHUNK
    cj = lax.broadcasted_iota(jnp.int32, (GLA_HEADS * GLA_CHUNK, GLA_CHUNK), 1)
    lower = cj <= ri

    dot = functools.partial(jnp.dot, preferred_element_type=F32)
    nt = (((1,), (1,)), ((), ()))

    def finish(o, r):
        s0, s1, s2 = _split3(o * o)
        ms = dot(s0, seg) + (dot(s1, seg) + dot(s2, seg))
        return (o * lax.rsqrt(ms + NORM_EPS) * gn * (r * jax.nn.sigmoid(r))).astype(o_ref.dtype)

    def elementwise_step(c, carry):
        rows = pl.ds(pl.multiple_of(c * GLA_CHUNK, GLA_CHUNK), GLA_CHUNK)
        q, k, v = q_ref[rows, :], k_ref[rows, :], v_ref[rows, :]
        b2f = _chunk_cumsum(tril, gf_ref[rows, :] * log2e)
        b2b = _chunk_cumsum(triu, gb_ref[rows, :] * log2e)
        qd = jnp.concatenate([q * jnp.exp2(b2f), q * jnp.exp2(b2b)], axis=-1).astype(BF16)
        st = jnp.concatenate([sf_ref[c], sb_ref[c]], axis=-1)
        o = lax.dot_general(qd, st, nt, preferred_element_type=F32)
        o = o + _gla_intra(q, k, v, b2f, b2b, e3, hmask, vmask)
        o_ref[rows, :] = finish(o, r_ref[rows, :])
        return carry

    def factored_block():
        chunks = [slice(c * GLA_CHUNK, (c + 1) * GLA_CHUNK) for c in range(nc)]
        g2f, g2b = gf_ref[...] * log2e, gb_ref[...] * log2e
        q, k, v16 = q_ref[...], k_ref[...], v_ref[...].astype(BF16)
        b2f = [_chunk_cumsum(tril, g2f[r]) for r in chunks]
        b2b = [_chunk_cumsum(triu, g2b[r]) for r in chunks]
        qf = [q[r] * jnp.exp2(b) for r, b in zip(chunks, b2f)]
        qb = [q[r] * jnp.exp2(b) for r, b in zip(chunks, b2b)]
        inter = [lax.dot_general(jnp.concatenate([qf[c], qb[c]], axis=-1).astype(BF16),
                                 jnp.concatenate([sf_ref[c], sb_ref[c]], axis=-1), nt,
                                 preferred_element_type=F32) for c in range(nc)]
        intra = _gla_intra_factored(qf, qb, [k[r] for r in chunks], [v16[r] for r in chunks], b2f, b2b,
                                    hmask, vmask, lower)
        o = jnp.concatenate([a + b for a, b in zip(inter, intra)], axis=0)
        o_ref[...] = finish(o, r_ref[...])

    gmin = jnp.minimum(jnp.min(gf_ref[...]), jnp.min(gb_ref[...]))
    mild = gmin >= GLA_FACTORED_MIN_GATE

    @pl.when(mild)
    def _():
        factored_block()

    @pl.when(jnp.logical_not(mild))
    def _():
        lax.fori_loop(0, nc, elementwise_step, 0, unroll=4)


def _gla(gq, gk, gv, gf, gb, gr, head_norm, *, tb=512):
    b, s, kw = gq.shape
    vw = gv.shape[-1]
    nb = s // tb
    ncb = tb // GLA_CHUNK
    dk = kw // GLA_HEADS
    dv = vw // GLA_HEADS
    idx = np.arange(GLA_CHUNK)
    tril = jnp.asarray((idx[None, :] <= idx[:, None]).astype(np.float32)).astype(BF16)
    triu = jnp.asarray((idx[None, :] >= idx[:, None]).astype(np.float32)).astype(BF16)
    d_head = np.arange(kw) // dk
    e_head = np.arange(vw) // dv
    e3 = jnp.asarray((d_head[:, None] == e_head[None, :]).astype(np.float32)).astype(BF16)
    hmask = jnp.asarray((np.arange(GLA_HEADS)[:, None] == d_head[None, :]).astype(np.float32))
    vmask = jnp.asarray((np.arange(GLA_HEADS)[:, None] == e_head[None, :]).astype(np.float32))
    stmask = jnp.asarray((e_head[:, None] == d_head[None, :]).astype(np.float32))
    seg = jnp.asarray((e_head[:, None] == e_head[None, :]).astype(np.float32) / dv).astype(BF16)
    tri_spec = _const_spec((GLA_CHUNK, GLA_CHUNK))

    fwd = lambda w: pl.BlockSpec((None, tb, w), lambda bi, i: (bi, i, 0))
    bwd = lambda w: pl.BlockSpec((None, tb, w), lambda bi, i: (bi, nb - 1 - i, 0))
    st_shape = jax.ShapeDtypeStruct((b, s // GLA_CHUNK, vw, kw), BF16)
    s_f, s_b = pl.pallas_call(
        _gla_state_kernel,
        grid=(b, nb),
        in_specs=[fwd(kw), fwd(vw), fwd(kw), bwd(kw), bwd(vw), bwd(kw), tri_spec, tri_spec,
                  _const_spec((vw, kw))],
        out_specs=[pl.BlockSpec((None, ncb, vw, kw), lambda bi, i: (bi, i, 0, 0)),
                   pl.BlockSpec((None, ncb, vw, kw), lambda bi, i: (bi, nb - 1 - i, 0, 0))],
        out_shape=[st_shape, st_shape],
        scratch_shapes=[pltpu.VMEM((vw, kw), F32), pltpu.VMEM((vw, kw), F32)],
        compiler_params=_cparams("parallel", "arbitrary"),
        name="gla_state",
    )(gk, gv, gf, gk, gv, gb, tril, triu, stmask)

    st_spec = pl.BlockSpec((None, ncb, vw, kw), lambda bi, i: (bi, i, 0, 0))
    return pl.pallas_call(
        _gla_out_kernel,
        grid=(b, nb),
        in_specs=[fwd(kw), fwd(kw), fwd(vw), fwd(kw), fwd(kw), fwd(vw), st_spec, st_spec,
                  tri_spec, tri_spec, _const_spec((kw, vw)), _const_spec((GLA_HEADS, kw)),
                  _const_spec((GLA_HEADS, vw)), _const_spec((vw, vw)), _const_spec((1, vw))],
        out_specs=fwd(vw),
        out_shape=jax.ShapeDtypeStruct((b, s, vw), BF16),
        compiler_params=_cparams("parallel", "parallel"),
        name="gla_out",
    )(gq, gk, gv, gf, gb, gr, s_f, s_b, tril, triu, e3, hmask, vmask, seg,
      jnp.tile(head_norm, GLA_HEADS).reshape(1, vw))


def _short_conv_kernel(*refs, p, pitch):
    nt = len(refs) - 4
    x_refs, (w_ref, b_ref, o_ref, s_ref) = refs[:nt], refs[nt:]
    n = x_refs[0].shape[0]
    hp = n // p
    c = nt * LANES
    rowi = lax.broadcasted_iota(jnp.int32, (hp, LANES), 0)
    for t, x_ref in enumerate(x_refs):
        for n2 in range(hp):
            s_ref[t, n2 * pitch:n2 * pitch + p, :] = x_ref[n2 * p:(n2 + 1) * p, :]
    for t in range(nt):
        lanes = slice(t * LANES, (t + 1) * LANES)
        w0, w1, w2, bias = w_ref[0:1, lanes], w_ref[1:2, lanes], w_ref[2:3, lanes], b_ref[:, lanes]
        def slab(n1):
            return s_ref[t, pl.ds(n1, hp, stride=pitch), :]

        first = slab(0)
        prev = jnp.where(rowi == 0, 0.0, pltpu.roll(slab(p - 1), 1, axis=0))
        cur = first
        for n1 in range(p):
            if n1 < p - 1:
                nxt = slab(n1 + 1)
            else:
                nxt = jnp.where(rowi == hp - 1, 0.0, pltpu.roll(first, hp - 1, axis=0))
            col = n1 * c + t * LANES
            o_ref[:, col:col + LANES] = bias + prev * w0 + cur * w1 + nxt * w2
            prev, cur = cur, nxt


def _short_conv(hy_in, conv_w, conv_b, p):
    b, s, c3 = hy_in.shape
    c = c3 // 3
    nt = c // LANES
    x_spec = lambda t: pl.BlockSpec((None, s, LANES), lambda bi, j: (bi, 0, j * nt + t))
    pitch = p + 4
    return pl.pallas_call(
        functools.partial(_short_conv_kernel, p=p, pitch=pitch),
        grid=(b, 3),
        in_specs=[x_spec(t) for t in range(nt)] + [
            pl.BlockSpec((HY_SHORT, c), lambda bi, j: (0, j)),
            pl.BlockSpec((1, c), lambda bi, j: (0, j)),
        ],
        out_specs=pl.BlockSpec((None, None, s // p, p * c), lambda bi, j: (j, bi, 0, 0)),
        out_shape=jax.ShapeDtypeStruct((3, b, s // p, p * c), F32),
        scratch_shapes=[pltpu.VMEM((nt, (s // p) * pitch, LANES), F32)],
        compiler_params=_cparams("parallel", "parallel"),
        name="hy_short_conv",
    )(*([hy_in] * nt), conv_w, conv_b.reshape(1, c3))


def _filter_kernel(band_ref, w1_ref, b1_ref, f1_ref, w2_ref, b2_ref, f2_ref, w3_ref, delta_ref, o_ref,
                   *, seq):
    tr = o_ref.shape[1]
    th = tr // 2
    half = LANES // 2
    c = delta_ref.shape[-1]
    oc = HY_ORDER * c
    row = pl.program_id(0) * tr + lax.broadcasted_iota(jnp.int32, (th, LANES), 0)
    lane = lax.broadcasted_iota(jnp.int32, (th, LANES), 1)

    def position(n):
        return jnp.where(n < seq, n, 2 * seq - n).astype(F32)

    pos_top, pos_bot = position(row), position(row + th)
    pos = jnp.where(lane < half, pos_top, pos_bot)
    ang = (2.0 * math.pi * pos / seq) * band_ref[0:1, :] + band_ref[1:2, :]
    feat = lane % half
    z = jnp.where(feat == 0, pos / (seq - 1), jnp.where(feat <= 2 * HY_BANDS, jnp.cos(ang), 0.0))
    h = jnp.sin(f1_ref[...] * (jnp.dot(z, w1_ref[...], precision=HIGHEST, preferred_element_type=F32)
                               + b1_ref[...]))
    h = jnp.sin(f2_ref[...] * (jnp.dot(h, w2_ref[...], precision=HIGHEST, preferred_element_type=F32)
                               + b2_ref[...]))
    h = jnp.dot(h, w3_ref[...], precision=HIGHEST, preferred_element_type=F32)
    for part, (n, p) in enumerate(((row, pos_top), (row + th, pos_bot))):
        t_c = jnp.concatenate([p / (seq - 1)] * (c // LANES), axis=-1)
        window = jnp.exp(-t_c * delta_ref[...]) + HY_DECAY_SHIFT
        live = jnp.concatenate([n] * (c // LANES), axis=-1) != seq
        for o in range(HY_ORDER):
            lo = part * oc + o * c
            o_ref[o, part * th:(part + 1) * th, :] = jnp.where(live, h[:, lo:lo + c] * window, 0.0)


def _block_diag2(w):
    z = jnp.zeros_like(w)
    return jnp.concatenate([jnp.concatenate([w, z], axis=1), jnp.concatenate([z, w], axis=1)], axis=0)


def _hyena_filters(seq, w1, b1, f1, w2, b2, f2, w3, *, tr=512):
    emb, hid = w1.shape
    half = LANES // 2
    assert hid == half and emb <= half
    c = w3.shape[1] // (HY_ORDER * 2)
    n = 2 * seq
    bands = np.zeros((2, half), np.float32)
    bands[1, 1 + HY_BANDS:1 + 2 * HY_BANDS] = 0.5 * np.pi
    bvals = np.linspace(1e-4, HY_BANDS - 1, HY_BANDS, dtype=np.float32)
    bands[0, 1:1 + HY_BANDS] = bvals
    bands[0, 1 + HY_BANDS:1 + 2 * HY_BANDS] = bvals
    bands = np.concatenate([bands, bands], axis=1)
    max_decay = math.log(HY_DECAY_TARGET) / HY_FAST_DECAY_PCT
    min_decay = math.log(HY_DECAY_TARGET) / HY_SLOW_DECAY_PCT
    deltas = np.abs(np.linspace(min_decay, max_decay, c, dtype=np.float32)).reshape(1, c)
    twice = lambda a: jnp.concatenate([a, a]).reshape(1, LANES)
    w1d = _block_diag2(jnp.pad(w1, ((0, half - emb), (0, 0))))
    w3d = w3.reshape(hid, HY_ORDER, 2, c).transpose(2, 0, 1, 3).reshape(2, hid, HY_ORDER * c)
    w3d = jnp.stack([_block_diag2(w3d[0]), _block_diag2(w3d[1])])
    half_tiles = seq // tr
    kern = functools.partial(_filter_kernel, seq=seq)
    return pl.pallas_call(
        kern,
        grid=(n // tr,),
        in_specs=[
            _const_spec((2, LANES)),
            _const_spec((LANES, LANES)), _const_spec((1, LANES)), _const_spec((1, LANES)),
            _const_spec((LANES, LANES)), _const_spec((1, LANES)), _const_spec((1, LANES)),
            pl.BlockSpec((None, LANES, 2 * HY_ORDER * c), lambda i: (i // half_tiles, 0, 0)),
            _const_spec((1, c)),
        ],
        out_specs=pl.BlockSpec((HY_ORDER, tr, c), lambda i: (0, i, 0)),
        out_shape=jax.ShapeDtypeStruct((HY_ORDER, n, c), F32),
        compiler_params=_cparams("parallel"),
        name="hy_filter",
    )(jnp.asarray(bands), w1d, twice(b1), twice(f1), _block_diag2(w2), twice(b2), twice(f2), w3d,
      jnp.asarray(deltas))


def _dft_tables(p):
    idx = np.arange(p)
    ang = 2.0 * np.pi * np.outer(idx, idx) / p
    fr, fi = np.cos(ang), -np.sin(ang)
    tang = 2.0 * np.pi * np.outer(idx, idx) / (p * p)
    twr, twi = np.cos(tang), -np.sin(tang)
    return fr, fi, twr, twi


def _dft_dot(mat, data):
    return jnp.dot(mat.astype(BF16), data.astype(BF16), preferred_element_type=F32)


def _stage_a_kernel(x_ref, m_ref, o_ref):
    x = jnp.concatenate([x_ref[0], x_ref[1]], axis=0)
    o_ref[0] = _dft_dot(m_ref[...], x).astype(o_ref.dtype)


def _stage_a(x4, idx, mat, *, ct=2048):
    _, g2, hp, cols = x4.shape
    rows = mat.shape[0]
    return pl.pallas_call(
        _stage_a_kernel,
        grid=(g2 // 2, cols // ct),
        in_specs=[
            pl.BlockSpec((None, 2, hp, ct), lambda g, j: (idx, g, 0, j)),
            _const_spec(mat.shape),
        ],
        out_specs=pl.BlockSpec((1, rows, ct), lambda g, j: (g, 0, j)),
        out_shape=jax.ShapeDtypeStruct((g2 // 2, rows, cols), BF16),
        compiler_params=_cparams("parallel", "parallel"),
        name="hy_dft_a",
    )(x4, mat)


def _fwd_matrix(fr, fi, twr, twi):
    gr = fr * twr - fi * twi
    gi = fr * twi + fi * twr
    return jnp.concatenate([jnp.concatenate([gr, -gi], axis=1), jnp.concatenate([gi, gr], axis=1)], axis=0)


def _stack_pairs(a_ref, j):
    g = a_ref.shape[0]
    re = jnp.concatenate([a_ref[i, 0, j] for i in range(g)], axis=-1)
    im = jnp.concatenate([a_ref[i, 1, j] for i in range(g)], axis=-1)
    return jnp.concatenate([re, im], axis=0)


def _stage_c_kernel(a_ref, fr_ref, fi_ref, twr_ref, twi_ref, o_ref):
    g, _, kk, p, c = a_ref.shape
    fr, fi = fr_ref[...], fi_ref[...]
    xs = [_dft_dot(_fwd_matrix(fr, fi, twr_ref[j], twi_ref[j]), _stack_pairs(a_ref, j)).astype(o_ref.dtype)
          for j in range(kk)]
    for j, x in enumerate(xs):
        for i in range(g):
            o_ref[i, 0, j] = x[:p, i * c:(i + 1) * c]
            o_ref[i, 1, j] = x[p:, i * c:(i + 1) * c]


def _conv_core_kernel(a_ref, kh_ref, fr_ref, fi_ref, twr_ref, twi_ref, twbr_ref, twbi_ref, o_ref):
    g, _, kk, p, c = a_ref.shape
    fr, fi = fr_ref[...], fi_ref[...]
    xs = [_dft_dot(_fwd_matrix(fr, fi, twr_ref[j], twi_ref[j]), _stack_pairs(a_ref, j))
          for j in range(kk)]
    ys = []
    for j, x in enumerate(xs):
        xr, xi = x[:p], x[p:]
        kr = jnp.concatenate([kh_ref[0, j]] * g, axis=-1)
        ki = jnp.concatenate([kh_ref[1, j]] * g, axis=-1)
        ys.append(jnp.concatenate([xr * kr - xi * ki, xr * ki + xi * kr], axis=0))
    rs = []
    for j, y in enumerate(ys):
        br = fr * twbr_ref[j] - fi * twbi_ref[j]
        bi = -(fr * twbi_ref[j] + fi * twbr_ref[j])
        bs = jnp.concatenate([jnp.concatenate([br, -bi], axis=1), jnp.concatenate([bi, br], axis=1)], axis=0)
        rs.append(_dft_dot(bs, y).astype(o_ref.dtype))
    for j, r in enumerate(rs):
        for i in range(g):
            o_ref[i, 0, j] = r[:p, i * c:(i + 1) * c]
            o_ref[i, 1, j] = r[p:, i * c:(i + 1) * c]


def _tw_specs(p, kk):
    row = pl.BlockSpec((kk, 1, p), lambda i: (i, 0, 0))
    return row


def _stage_c(a5, tabs, *, kk=8):
    g, _, p, _, c = a5.shape
    fr, fi, twr, twi = tabs
    blk = pl.BlockSpec((g, 2, kk, p, c), lambda i: (0, 0, i, 0, 0))
    return pl.pallas_call(
        _stage_c_kernel,
        grid=(p // kk,),
        in_specs=[blk, _const_spec((p, p)), _const_spec((p, p)), _tw_specs(p, kk), _tw_specs(p, kk)],
        out_specs=blk,
        out_shape=jax.ShapeDtypeStruct(a5.shape, BF16),
        compiler_params=_cparams("parallel"),
        name="hy_dft_c",
    )(a5, fr, fi, twr.reshape(p, 1, p), twi.reshape(p, 1, p))


def _conv_core(a5, khat, order, tabs, twb, *, kk=8):
    g, _, p, _, c = a5.shape
    fr, fi, twr, twi = tabs
    twbr, twbi = twb
    blk = pl.BlockSpec((g, 2, kk, p, c), lambda i: (0, 0, i, 0, 0))
    return pl.pallas_call(
        _conv_core_kernel,
        grid=(p // kk,),
        in_specs=[
            blk,
            pl.BlockSpec((None, 2, kk, p, c), lambda i: (order, 0, i, 0, 0)),
            _const_spec((p, p)), _const_spec((p, p)), _tw_specs(p, kk), _tw_specs(p, kk),
            pl.BlockSpec((kk, p, p), lambda i: (i, 0, 0)),
            pl.BlockSpec((kk, p, p), lambda i: (i, 0, 0)),
        ],
        out_specs=blk,
        out_shape=jax.ShapeDtypeStruct(a5.shape, BF16),
        compiler_params=_cparams("parallel"),
        name="hy_conv_core",
    )(a5, khat, fr, fi, twr.reshape(p, 1, p), twi.reshape(p, 1, p), twbr, twbi)


def _stage_b_kernel(r_ref, m_ref, u_ref, gate_ref, skip_ref, ma_ref, o_ref, *next_ref):
    y = _dft_dot(m_ref[...], r_ref[0])
    hp = u_ref.shape[1]
    skip = skip_ref[...]
    z0 = gate_ref[0] * (y[:hp] + skip * u_ref[0])
    z1 = gate_ref[1] * (y[hp:] + skip * u_ref[1])
    o_ref[0] = z0
    o_ref[1] = z1
    if next_ref:
        x = jnp.concatenate([z0, z1], axis=0)
        next_ref[0][0] = _dft_dot(ma_ref[...], x).astype(next_ref[0].dtype)


def _stage_b(r3, mat, u4, u_idx, gate4, gate_idx, skip_row, mat_a, *, emit_next, ct=2048):
    g, rows, cols = r3.shape
    hp = mat.shape[0] // 2
    nb = u4.shape[1]
    out_specs = [pl.BlockSpec((2, hp, ct), lambda gi, j: (gi, 0, j))]
    out_shape = [jax.ShapeDtypeStruct((nb, hp, cols), F32)]
    if emit_next:
        out_specs.append(pl.BlockSpec((1, mat_a.shape[0], ct), lambda gi, j: (gi, 0, j)))
        out_shape.append(jax.ShapeDtypeStruct((g, mat_a.shape[0], cols), BF16))
    return pl.pallas_call(
        _stage_b_kernel,
        grid=(g, cols // ct),
        in_specs=[
            pl.BlockSpec((1, rows, ct), lambda gi, j: (gi, 0, j)),
            _const_spec(mat.shape),
            pl.BlockSpec((None, 2, hp, ct), lambda gi, j: (u_idx, gi, 0, j)),
            pl.BlockSpec((None, 2, hp, ct), lambda gi, j: (gate_idx, gi, 0, j)),
            pl.BlockSpec((1, ct), lambda gi, j: (0, j)),
            _const_spec(mat_a.shape),
        ],
        out_specs=out_specs,
        out_shape=out_shape,
        compiler_params=_cparams("parallel", "parallel"),
        name="hy_dft_b",
    )(r3, mat, u4, gate4, skip_row, mat_a)


def _hyena(hy_in, conv_w, conv_b, w1, b1, f1, w2, b2, f2, w3, skip):
    b, s, c3 = hy_in.shape
    c = c3 // 3
    p = math.isqrt(2 * s)
    assert p * p == 2 * s and b % 2 == 0
    hp = p // 2
    cols = p * c
    fr, fi, twr, twi = _dft_tables(p)
    scale = 1.0 / (p * p)
    mat_a = np.block([[fr[:, :hp], -fi[:, :hp]], [fi[:, :hp], fr[:, :hp]]]).astype(np.float32)
    mat_k = np.concatenate([fr, fi], axis=0).astype(np.float32)
    br, bi = fr[:hp] * scale, -fi[:hp] * scale
    mat_b = np.block([[br, -bi], [bi, br]]).astype(np.float32)
    tabs = tuple(jnp.asarray(t.astype(np.float32)) for t in (fr, fi, twr, twi))
    twb = tuple(jnp.broadcast_to(t[:, :, None], (p, p, p)) for t in tabs[2:])

    uvx = _short_conv(hy_in, conv_w, conv_b, p)
    kern = _hyena_filters(s, w1, b1, f1, w2, b2, f2, w3)
    ka = _stage_a(kern.reshape(1, HY_ORDER * 2, hp, cols), 0, jnp.asarray(mat_k))
    khat = _stage_c(ka.reshape(HY_ORDER, 2, p, p, c), tabs)

    skip_rows = jnp.tile(skip, (1, p))
    g = b // 2

    ma, mb = jnp.asarray(mat_a), jnp.asarray(mat_b)

    def long_conv(a, u4, u_idx, gate_idx, order, emit_next):
        r = _conv_core(a.reshape(g, 2, p, p, c), khat, order, tabs, twb)
        return _stage_b(r.reshape(g, 2 * p, cols), mb, u4, u_idx, uvx, gate_idx,
                        skip_rows[order:order + 1], ma, emit_next=emit_next)

    a1 = _stage_a(uvx, 0, ma)
    z, a2 = long_conv(a1, uvx, 0, 1, 0, True)
    (y,) = long_conv(a2, z[None], 0, 2, 1, False)
    return y.reshape(b, s, c)


def kernel(x, ffn1_norm, ffn1_w_gate, ffn1_w_up, ffn1_w_down, mix_norm, w_in, hy_conv_w, hy_conv_b, hy_filt_w1, hy_filt_b1, hy_filt_freq1, hy_filt_w2, hy_filt_b2, hy_filt_freq2, hy_filt_w3, hy_skip, hy_out_norm, mla_q_norm, mla_w_uq, mla_kv_norm, mla_w_ukv, mla_out_norm, gla_w_gate_fwd, gla_b_gate_fwd, gla_w_gate_bwd, gla_b_gate_bwd, gla_head_norm, w_out, ffn2_norm, ffn2_w_gate, ffn2_w_up, ffn2_w_down, final_norm):
    b, s, d = x.shape
    depth = ffn1_norm.shape[0]
    t = b * s
    for l in range(depth):
        x2 = _ffn(x.reshape(t, d), ffn1_norm[l], ffn1_w_gate[l], ffn1_w_up[l], ffn1_w_down[l], final_norm,
                  final=False)
        (hy_in, q, k, v, gq, gk, gv, gf, gb, gr) = _in_proj(
            x2.reshape(b, s, d), mix_norm[l], w_in[l], mla_q_norm[l], mla_w_uq[l], mla_kv_norm[l],
            mla_w_ukv[l], gla_w_gate_fwd[l], gla_b_gate_fwd[l], gla_w_gate_bwd[l], gla_b_gate_bwd[l])
        y_hy = _hyena(hy_in, hy_conv_w[l], hy_conv_b[l], hy_filt_w1[l], hy_filt_b1[l], hy_filt_freq1[l],
                      hy_filt_w2[l], hy_filt_b2[l], hy_filt_freq2[l], hy_filt_w3[l], hy_skip[l])
        o_mla = _mla_attention(q, k, v)
        y_gla = _gla(gq, gk, gv, gf, gb, gr, gla_head_norm[l])
        mix = (y_hy.reshape(t, -1), o_mla.reshape(t, -1), y_gla.reshape(t, -1), hy_out_norm[l],
               mla_out_norm[l], w_out[l])
        x = _ffn(x2, ffn2_norm[l], ffn2_w_gate[l], ffn2_w_up[l], ffn2_w_down[l], final_norm,
                 final=(l == depth - 1), mix=mix).reshape(b, s, d)
    return x
```
